```python
import math
import jax, jax.numpy as jnp
from jax import lax
import numpy as np

D_MODEL = 2048
BATCH = 32
SEQ = 256
DEPTH = 4
DEC_BATCH = 8
DEC_SEQ = 2048
PAST_LEN = 256

GRID_W = 64
N_MOD = 9
D_FF = 5632
BRANCH_WIDTH = D_MODEL // 2
D_CONV = BRANCH_WIDTH
N_HEADS = 8
DQK = 64
DV = 2 * DQK
D_ATT_QK = N_HEADS * 2 * DQK
D_ATT_V = N_HEADS * DV
D_GMLP = BRANCH_WIDTH
GMLP_GROUPS = 8
CHUNK = 128
Q_BLOCK = 128
N_BRANCH = 3
ROPE_THETA = 10000.0
AXIS_DIM = DQK // 2
EPS = 1e-6
SPLIT_SIZES = (D_CONV, D_CONV, D_CONV, D_ATT_QK, D_ATT_QK, D_ATT_V, D_GMLP, D_GMLP, N_BRANCH * D_MODEL)
IN_WIDTH = sum(SPLIT_SIZES)
SPLIT_POINTS = tuple(int(s) for s in np.cumsum(SPLIT_SIZES)[:-1])

kernel_name = 'hybrid_diffusion_gated_branch_step'


def rms_norm(x, gain):
    xf = x.astype(jnp.float32)
    y = xf * lax.rsqrt(jnp.mean(xf * xf, axis=-1, keepdims=True) + EPS)
    return (y * gain.astype(jnp.float32)).astype(x.dtype)


def layer_norm(x, gain, bias):
    xf = x.astype(jnp.float32)
    mu = jnp.mean(xf, axis=-1, keepdims=True)
    var = jnp.mean(jnp.square(xf - mu), axis=-1, keepdims=True)
    y = (xf - mu) * lax.rsqrt(var + EPS)
    return (y * gain.astype(jnp.float32) + bias.astype(jnp.float32)).astype(x.dtype)


def modulate(x, gain, shift, scale):
    return rms_norm(x, gain) * (1 + scale[:, None, :]) + shift[:, None, :]


def adaln(cvec, w, b):
    return (jax.nn.silu(cvec) @ w + b).reshape(cvec.shape[0], N_MOD, D_MODEL)


def swiglu(h, w_in, w_out):
    a, g = jnp.split(h @ w_in, 2, axis=-1)
    return (jax.nn.silu(a) * g) @ w_out


def short_conv(z, w):
    zp = jnp.pad(z, ((0, 0), (1, 1), (0, 0)))
    return w[0] * zp[:, :-2] + w[1] * zp[:, 1:-1] + w[2] * zp[:, 2:]


def axial_rope_tables(rows):
    row = jnp.repeat(jnp.arange(rows, dtype=jnp.float32), GRID_W)
    col = jnp.tile(jnp.arange(GRID_W, dtype=jnp.float32), rows)
    inv = ROPE_THETA ** (-jnp.arange(0, AXIS_DIM, 2, dtype=jnp.float32) / AXIS_DIM)
    ang = jnp.stack([row[:, None] * inv, col[:, None] * inv], axis=1)
    return jnp.cos(ang), jnp.sin(ang)


def apply_axial_rope(x, cos, sin):
    shp = x.shape
    xr = x.reshape(shp[:-1] + (2, 2, AXIS_DIM // 2))
    x1, x2 = xr[..., 0, :], xr[..., 1, :]
    c = cos[None, :, None, None].astype(x.dtype)
    s = sin[None, :, None, None].astype(x.dtype)
    out = jnp.stack([x1 * c - x2 * s, x1 * s + x2 * c], axis=-2)
    return out.reshape(shp)


def diff_attention(q, k, v, lam):
    B, S, H, _, Dk = q.shape
    nb = S // Q_BLOCK
    qb = q.reshape(B, nb, Q_BLOCK, H, 2, Dk).transpose(1, 0, 2, 3, 4, 5)
    scale = Dk ** -0.5

    def block(qi):
        s = jnp.einsum('bqhcd,bkhcd->bhcqk', qi, k, preferred_element_type=jnp.float32) * scale
        p = jax.nn.softmax(s, axis=-1)
        w = p[:, :, 0] - lam * p[:, :, 1]
        return jnp.einsum('bhqk,bkhe->bqhe', w.astype(v.dtype), v)

    o = lax.map(block, qb)
    return o.transpose(1, 0, 2, 3, 4).reshape(B, S, H, v.shape[-1])


def chunk_gmlp(u, v, ln_g, ln_b, w_s, b_s):
    B, S, C = v.shape
    vn = layer_norm(v, ln_g, ln_b).reshape(B, S // CHUNK, CHUNK, GMLP_GROUPS, C // GMLP_GROUPS)
    mixed = jnp.einsum('gpq,bnqgc->bnpgc', w_s, vn) + b_s.T[None, None, :, :, None]
    return u * mixed.reshape(B, S, C)


def token_mixer(h, l, mix, rope, ctx_k, ctx_v):
    (w_in, w_conv, qk_gain, lam, subln_g, gmlp_ln_g, gmlp_ln_b, w_s, b_s, w_branch, w_o) = mix
    B, S, _ = h.shape
    z = h @ w_in[l]
    xa, gb, gc, q, k, v, u, vg, zg = jnp.split(z, SPLIT_POINTS, axis=-1)
    ya = (gb * short_conv(gc * xa, w_conv[l])) @ w_branch[l, 0]
    q = rms_norm(q.reshape(B, S, N_HEADS, 2, DQK), qk_gain[l, 0])
    k = rms_norm(k.reshape(B, S, N_HEADS, 2, DQK), qk_gain[l, 1])
    v = v.reshape(B, S, N_HEADS, DV)
    lam_init = 0.8 - 0.6 * math.exp(-0.3 * l)
    lp = lam[l].astype(jnp.float32)
    lam_full = jnp.exp(jnp.sum(lp[0] * lp[1])) - jnp.exp(jnp.sum(lp[2] * lp[3])) + lam_init
    if rope is None:
        k_all, v_all = k, v
    else:
        cos, sin = rope
        q = apply_axial_rope(q, cos, sin)
        k_lat = apply_axial_rope(k, cos, sin)
        k_all = jnp.concatenate([ctx_k.astype(k.dtype), k_lat], axis=1)
        v_all = jnp.concatenate([ctx_v.astype(v.dtype), v], axis=1)
    o = diff_attention(q, k_all, v_all, lam_full)
    o = rms_norm(o, subln_g[l]) * (1 - lam_init)
    yb = o.reshape(B, S, D_ATT_V) @ w_branch[l, 1]
    yc = chunk_gmlp(jax.nn.gelu(u), jax.nn.gelu(vg), gmlp_ln_g[l], gmlp_ln_b[l], w_s[l], b_s[l]) @ w_branch[l, 2]
    g = jax.nn.sigmoid(zg).reshape(B, S, N_BRANCH, D_MODEL)
    merged = g[..., 0, :] * ya + g[..., 1, :] * yb + g[..., 2, :] * yc
    return merged @ w_o[l], k, v


def trunk_layer(x, mod, l, norm_g, w_ffn_in, w_ffn_out, mix, rope, ctx_k, ctx_v):
    h = modulate(x, norm_g[l, 0], mod[:, 0], mod[:, 1])
    x = x + 0.5 * mod[:, 2][:, None, :] * swiglu(h, w_ffn_in[l, 0], w_ffn_out[l, 0])
    h = modulate(x, norm_g[l, 1], mod[:, 3], mod[:, 4])
    y, k, v = token_mixer(h, l, mix, rope, ctx_k, ctx_v)
    x = x + mod[:, 5][:, None, :] * y
    h = modulate(x, norm_g[l, 2], mod[:, 6], mod[:, 7])
    x = x + 0.5 * mod[:, 8][:, None, :] * swiglu(h, w_ffn_in[l, 1], w_ffn_out[l, 1])
    return x, k, v


def _normal(k, shape, scale):
    return jax.random.normal(k, shape, jnp.float32) * scale


def setup_inputs(seed: int = 0) -> dict:
    key = jax.random.key(seed)
    ks = jax.random.split(key, 24)
    D = D_MODEL
    return {
        'x_prompt': _normal(ks[0], (BATCH, SEQ, D), 1.0),
        'x_sample': _normal(ks[1], (DEC_BATCH, DEC_SEQ, D), 1.0),
        'c': _normal(ks[2], (DEC_BATCH, D), 1.0),
        'c_ctx': _normal(ks[3], (D,), 1.0),
        'cache_k': _normal(ks[4], (DEC_BATCH, DEPTH, PAST_LEN, N_HEADS, 2, DQK), 1.0),
        'cache_v': _normal(ks[5], (DEC_BATCH, DEPTH, PAST_LEN, N_HEADS, DV), 1.0),
        'w_mod': _normal(ks[6], (DEPTH, D, N_MOD * D), 0.5 * D ** -0.5),
        'b_mod': _normal(ks[7], (DEPTH, N_MOD * D), 0.01),
        'norm_g': 1.0 + _normal(ks[8], (DEPTH, 3, D), 0.02),
        'w_ffn_in': _normal(ks[9], (DEPTH, 2, D, 2 * D_FF), D ** -0.5),
        'w_ffn_out': _normal(ks[10], (DEPTH, 2, D_FF, D), D_FF ** -0.5),
        'w_in': _normal(ks[11], (DEPTH, D, IN_WIDTH), D ** -0.5),
        'w_conv': _normal(ks[12], (DEPTH, 3, D_CONV), 3 ** -0.5),
        'qk_gain': 1.0 + _normal(ks[13], (DEPTH, 2, DQK), 0.02),
        'lam': _normal(ks[14], (DEPTH, 4, DQK), 0.1),
        'subln_g': 1.0 + _normal(ks[15], (DEPTH, DV), 0.02),
        'gmlp_ln_g': 1.0 + _normal(ks[16], (DEPTH, D_GMLP), 0.02),
        'gmlp_ln_b': _normal(ks[17], (DEPTH, D_GMLP), 0.01),
        'w_s': _normal(ks[18], (DEPTH, GMLP_GROUPS, CHUNK, CHUNK), CHUNK ** -0.5),
        'b_s': 1.0 + _normal(ks[19], (DEPTH, GMLP_GROUPS, CHUNK), 0.01),
        'w_branch': _normal(ks[20], (DEPTH, N_BRANCH, BRANCH_WIDTH, D), BRANCH_WIDTH ** -0.5),
        'w_o': _normal(ks[21], (DEPTH, D, D), D ** -0.5),
    }


def reference(x_prompt, x_sample, c, c_ctx, cache_k, cache_v, w_mod, b_mod, norm_g, w_ffn_in, w_ffn_out,
              w_in, w_conv, qk_gain, lam, subln_g, gmlp_ln_g, gmlp_ln_b, w_s, b_s, w_branch, w_o):
    mix = (w_in, w_conv, qk_gain, lam, subln_g, gmlp_ln_g, gmlp_ln_b, w_s, b_s, w_branch, w_o)
    rows = x_sample.shape[1] // GRID_W
    rope = axial_rope_tables(rows)
    xp, xs = x_prompt, x_sample
    ks, vs = [], []
    for l in range(DEPTH):
        mod_ctx = adaln(c_ctx[None, :], w_mod[l], b_mod[l])
        mod_lat = adaln(c, w_mod[l], b_mod[l])
        xp, k_l, v_l = trunk_layer(xp, mod_ctx, l, norm_g, w_ffn_in, w_ffn_out, mix, None, None, None)
        xs, _, _ = trunk_layer(xs, mod_lat, l, norm_g, w_ffn_in, w_ffn_out, mix, rope,
                               cache_k[:, l], cache_v[:, l])
        ks.append(k_l)
        vs.append(v_l)
    new_cache_k = jnp.stack(ks, axis=1)
    new_cache_v = jnp.stack(vs, axis=1)
    return (xp, xs, new_cache_k, new_cache_v)
```

```python
import functools
import math

import jax
import jax.numpy as jnp
import numpy as np
from jax import lax
from jax.experimental import pallas as pl
from jax.experimental.pallas import tpu as pltpu

D_MODEL = 2048
GRID_W = 64
N_MOD = 9
D_FF = 5632
BRANCH_WIDTH = D_MODEL // 2
N_HEADS = 8
DQK = 64
DV = 2 * DQK
GMLP_GROUPS = 8
CHUNK = 128
N_BRANCH = 3
ROPE_THETA = 10000.0
AXIS_DIM = DQK // 2
EPS = 1e-6
N_IN_GROUPS = 8 + 2 * N_BRANCH
IN_WIDTH = N_IN_GROUPS * BRANCH_WIDTH
LANES = 128
MOD_ROWS = 16
HALO = 16
VMEM_LIMIT = 56 * 1024 * 1024

F32 = jnp.float32
BF16 = jnp.bfloat16


def _params(semantics):
    return pltpu.CompilerParams(dimension_semantics=semantics, vmem_limit_bytes=VMEM_LIMIT)


def _modulated_norm(x, gain, shift, scale):
    ms = jnp.mean(x * x, axis=-1, keepdims=True)
    return (x * lax.rsqrt(ms + EPS) * gain) * (1.0 + scale) + shift


def _mod_kernel(c_ref, w_ref, b_ref, o_ref):
    c = c_ref[...]
    a = (c * jax.nn.sigmoid(c)).astype(BF16)
    o_ref[...] = jnp.dot(a, w_ref[...].astype(BF16), preferred_element_type=F32) + b_ref[...]


def _mod_call(cvec, w_mod, b_mod):
    depth = w_mod.shape[0]
    tn = 1024
    nn = N_MOD * D_MODEL // tn
    return pl.pallas_call(
        _mod_kernel,
        grid=(depth, nn),
        in_specs=[
            pl.BlockSpec((MOD_ROWS, D_MODEL), lambda l, n: (0, 0)),
            pl.BlockSpec((None, D_MODEL, tn), lambda l, n: (l, 0, n)),
            pl.BlockSpec((None, 1, tn), lambda l, n: (l, 0, n)),
        ],
        out_specs=pl.BlockSpec((None, MOD_ROWS, tn), lambda l, n: (l, 0, n)),
        out_shape=jax.ShapeDtypeStruct((depth, MOD_ROWS, N_MOD * D_MODEL), F32),
        compiler_params=_params(("parallel", "parallel")),
        name="adaln_mod",
    )(cvec, w_mod, b_mod.reshape(depth, 1, N_MOD * D_MODEL))


def _mod_spec(l, j, row_of_tile):
    return pl.BlockSpec((None, None, None, 1, D_MODEL), lambda i, n: (l, row_of_tile(i), j, 0, 0))


def _ffn_kernel(x_ref, g_ref, sh_ref, sc_ref, gt_ref, wa_ref, wg_ref, wo_ref, o_ref, h_ref, *, nf):
    f = pl.program_id(1)

    @pl.when(f == 0)
    def _():
        h_ref[...] = _modulated_norm(x_ref[...], g_ref[...], sh_ref[...], sc_ref[...]).astype(BF16)

    h = h_ref[...]
    a = jnp.dot(h, wa_ref[...], preferred_element_type=F32)
    g = jnp.dot(h, wg_ref[...], preferred_element_type=F32)
    act = (a * jax.nn.sigmoid(a) * g).astype(BF16)
    part = jnp.dot(act, wo_ref[...], preferred_element_type=F32)

    @pl.when(f == 0)
    def _():
        o_ref[...] = part

    @pl.when(f > 0)
    def _():
        o_ref[...] += part

    @pl.when(f == nf - 1)
    def _():
        o_ref[...] = x_ref[...] + (0.5 * gt_ref[...]) * o_ref[...]


def _ffn_call(x, mod, norm_g, w_in, w_out, *, l, s, tm, tf, row_of_tile):
    t = x.shape[0]
    nf = D_FF // tf
    mod_base = 0 if s == 0 else 6
    norm_idx = 0 if s == 0 else 2
    return pl.pallas_call(
        functools.partial(_ffn_kernel, nf=nf),
        grid=(t // tm, nf),
        in_specs=[
            pl.BlockSpec((tm, D_MODEL), lambda i, f: (i, 0)),
            pl.BlockSpec((None, None, 1, D_MODEL), lambda i, f: (l, norm_idx, 0, 0)),
            _mod_spec(l, mod_base + 0, row_of_tile),
            _mod_spec(l, mod_base + 1, row_of_tile),
            _mod_spec(l, mod_base + 2, row_of_tile),
            pl.BlockSpec((None, None, D_MODEL, tf), lambda i, f: (l, s, 0, f)),
            pl.BlockSpec((None, None, D_MODEL, tf), lambda i, f: (l, s, 0, nf + f)),
            pl.BlockSpec((None, None, tf, D_MODEL), lambda i, f: (l, s, f, 0)),
        ],
        out_specs=pl.BlockSpec((tm, D_MODEL), lambda i, f: (i, 0)),
        out_shape=jax.ShapeDtypeStruct((t, D_MODEL), F32),
        scratch_shapes=[pltpu.VMEM((tm, D_MODEL), BF16)],
        compiler_params=_params(("parallel", "arbitrary")),
        name=f"ffn{s}",
    )(x, norm_g, mod, mod, mod, w_in, w_in, w_out)


def _segment_mean_square(z):
    zz = z * z
    hi = zz.astype(BF16)
    lo = (zz - hi.astype(F32)).astype(BF16)
    r = lax.broadcasted_iota(jnp.int32, (2 * LANES, LANES), 0)
    c = lax.broadcasted_iota(jnp.int32, (2 * LANES, LANES), 1)
    seg = jnp.where((r % LANES) // DQK == c // DQK, 1.0, 0.0).astype(BF16)
    ss = jnp.dot(jnp.concatenate([hi, lo], axis=1), seg, preferred_element_type=F32)
    return ss * (1.0 / DQK)


def _rotary(y, cos, sin_signed):
    lane = lax.broadcasted_iota(jnp.int32, y.shape, 1)
    ahead = pltpu.roll(y, LANES - AXIS_DIM // 2, axis=1)
    behind = pltpu.roll(y, AXIS_DIM // 2, axis=1)
    partner = jnp.where(lane % AXIS_DIM < AXIS_DIM // 2, ahead, behind)
    return y * cos + partner * sin_signed


def _qk_norm(z, gain, rope, out_scale):
    outs = []
    for c in range(BRANCH_WIDTH // LANES):
        zc = z[:, c * LANES:(c + 1) * LANES]
        y = zc * lax.rsqrt(_segment_mean_square(zc) + EPS) * gain[:, c * LANES:(c + 1) * LANES]
        if rope is not None:
            y = _rotary(y, rope[0], rope[1])
        if out_scale != 1.0:
            y = y * out_scale
        outs.append(y)
    return outs


def _mix_in_kernel(*refs, tm, seq, is_ctx):
    it = iter(refs)
    x_ref, g_ref, sh_ref, sc_ref, w_ref, qkg_ref = (next(it) for _ in range(6))
    cos_ref = sin_ref = None
    if not is_ctx:
        cos_ref, sin_ref = next(it), next(it)
    lng_ref, lnb_ref, ws_ref, bs_ref = (next(it) for _ in range(4))
    if is_ctx:
        next(it), next(it)
    p_ref, gb_ref, q_ref, k_ref, v_ref, yc_ref, gates_ref = (next(it) for _ in range(7))
    kc_ref = vc_ref = None
    if is_ctx:
        kc_ref, vc_ref = next(it), next(it)
    h_ref, xa_ref, ug_ref = next(it), next(it), next(it)

    j = pl.program_id(1)

    @pl.when(j == 0)
    def _():
        h_ref[...] = _modulated_norm(x_ref[...], g_ref[...], sh_ref[...], sc_ref[...]).astype(BF16)

    z = jnp.dot(h_ref[...], w_ref[...], preferred_element_type=F32)

    @pl.when(j == 0)
    def _():
        xa_ref[...] = z

    @pl.when(j == 1)
    def _():
        gb_ref[...] = z.astype(BF16)

    @pl.when(j == 2)
    def _():
        p_ref[...] = (z * xa_ref[...]).astype(BF16)

    def rope():
        return None if is_ctx else (cos_ref[...], sin_ref[...])

    @pl.when(j == 3)
    def _():
        ys = _qk_norm(z, qkg_ref[0], rope(), DQK ** -0.5)
        for c, y in enumerate(ys):
            q_ref[:, c * LANES:(c + 1) * LANES] = y.astype(BF16)

    @pl.when(j == 4)
    def _():
        if is_ctx:
            ys = _qk_norm(z, qkg_ref[1], None, 1.0)
            for c, y in enumerate(ys):
                k_ref[:, c * LANES:(c + 1) * LANES] = y.astype(BF16)
                for b in range(tm // seq):
                    kc_ref[b, :, c * LANES:(c + 1) * LANES] = y[b * seq:(b + 1) * seq]
        else:
            ys = _qk_norm(z, qkg_ref[1], rope(), 1.0)
            for c, y in enumerate(ys):
                k_ref[:, c * LANES:(c + 1) * LANES] = y.astype(BF16)

    @pl.when(j == 5)
    def _():
        v_ref[...] = z.astype(BF16)
        if is_ctx:
            for b in range(tm // seq):
                vc_ref[b] = z[b * seq:(b + 1) * seq]

    @pl.when(j == 6)
    def _():
        ug_ref[...] = jax.nn.gelu(z)

    @pl.when(j == 7)
    def _():
        gv = jax.nn.gelu(z)
        mu = jnp.mean(gv, axis=-1, keepdims=True)
        d = gv - mu
        var = jnp.mean(d * d, axis=-1, keepdims=True)
        vn = (d * lax.rsqrt(var + EPS) * lng_ref[...] + lnb_ref[...]).astype(BF16)
        nch = tm // CHUNK
        for g in range(GMLP_GROUPS):
            cols = slice(g * CHUNK, (g + 1) * CHUNK)
            stacked = jnp.concatenate([vn[n * CHUNK:(n + 1) * CHUNK, cols] for n in range(nch)], axis=1)
            mixed = jnp.dot(ws_ref[g], stacked, preferred_element_type=F32)
            for n in range(nch):
                rows = slice(n * CHUNK, (n + 1) * CHUNK)
                m = mixed[:, n * CHUNK:(n + 1) * CHUNK] + bs_ref[:, cols]
                yc_ref[rows, cols] = (ug_ref[rows, cols] * m).astype(BF16)

    @pl.when(j >= 8)
    def _():
        gates_ref[...] = jax.nn.sigmoid(z).astype(BF16)


def _mix_in_call(x, mod, norm_g, w_in, qk_gain_t, rope, ln_g, ln_b, w_s, b_s_t, caches, *, l, tm, seq, row_of_tile):
    t = x.shape[0]
    is_ctx = caches is not None
    bw = BRANCH_WIDTH
    tok = lambda i, j: (i, 0)
    in_specs = [
        pl.BlockSpec((tm, D_MODEL), tok),
        pl.BlockSpec((None, None, 1, D_MODEL), lambda i, j: (l, 1, 0, 0)),
        _mod_spec(l, 3, row_of_tile),
        _mod_spec(l, 4, row_of_tile),
        pl.BlockSpec((None, D_MODEL, bw), lambda i, j: (l, 0, j)),
        pl.BlockSpec((None, 2, 1, bw), lambda i, j: (l, 0, 0, 0)),
    ]
    args = [x, norm_g, mod, mod, w_in, qk_gain_t]
    if not is_ctx:
        tiles_per_seq = seq // tm
        in_specs += [pl.BlockSpec((tm, LANES), lambda i, j: (i % tiles_per_seq, 0))] * 2
        args += [rope[0], rope[1]]
    in_specs += [
        pl.BlockSpec((None, 1, bw), lambda i, j: (l, 0, 0)),
        pl.BlockSpec((None, 1, bw), lambda i, j: (l, 0, 0)),
        pl.BlockSpec((None, GMLP_GROUPS, CHUNK, CHUNK), lambda i, j: (l, 0, 0, 0)),
        pl.BlockSpec((None, CHUNK, bw), lambda i, j: (l, 0, 0)),
    ]
    args += [ln_g, ln_b, w_s, b_s_t]
    act = jax.ShapeDtypeStruct((t, bw), BF16)
    out_shape = [act] * 6 + [jax.ShapeDtypeStruct((t, 2 * N_BRANCH * bw), BF16)]
    out_specs = [pl.BlockSpec((tm, bw), tok)] * 6 + [
        pl.BlockSpec((tm, bw), lambda i, j: (i, jnp.maximum(j - 8, 0)))]
    aliases = {}
    if is_ctx:
        nb = tm // seq
        cache_spec = pl.BlockSpec((nb, None, seq, bw), lambda i, j: (i, l, 0, 0))
        in_specs += [pl.BlockSpec(memory_space=pl.ANY)] * 2
        aliases = {len(args): 7, len(args) + 1: 8}
        args += list(caches)
        out_shape += [jax.ShapeDtypeStruct(c.shape, c.dtype) for c in caches]
        out_specs += [cache_spec, cache_spec]
    return pl.pallas_call(
        functools.partial(_mix_in_kernel, tm=tm, seq=seq, is_ctx=is_ctx),
        grid=(t // tm, N_IN_GROUPS),
        in_specs=in_specs,
        out_specs=out_specs,
        out_shape=out_shape,
        input_output_aliases=aliases,
        scratch_shapes=[pltpu.VMEM((tm, D_MODEL), BF16), pltpu.VMEM((tm, bw), F32), pltpu.VMEM((tm, bw), F32)],
        compiler_params=_params(("parallel", "arbitrary")),
        name="mix_in_ctx" if is_ctx else "mix_in_lat",
    )(*args)


def _lambda_full(lam_ref, l):
    lp = lam_ref[...]
    a = jnp.sum(lp[0:1] * lp[1:2], axis=-1, keepdims=True)
    b = jnp.sum(lp[2:3] * lp[3:4], axis=-1, keepdims=True)
    return jnp.exp(a) - jnp.exp(b) + _lambda_init(l)


def _lambda_init(l):
    return 0.8 - 0.6 * math.exp(-0.3 * l)


def _softmax_times_values(qm, k, vaug):
    s = lax.dot_general(qm, k, (((1,), (1,)), ((), ())), preferred_element_type=F32)
    e = jnp.exp(s - jnp.max(s, axis=-1, keepdims=True)).astype(BF16)
    pv = jnp.dot(e, vaug, preferred_element_type=F32)
    return pv[:, :DV] / pv[:, DV:]


def _attend_head(qh, kh, vaug, lam_full, subg, l):
    lane = lax.broadcasted_iota(jnp.int32, qh.shape, 1)
    zero = jnp.zeros_like(qh)
    o0 = _softmax_times_values(jnp.where(lane < DQK, qh, zero), kh, vaug)
    o1 = _softmax_times_values(jnp.where(lane < DQK, zero, qh), kh, vaug)
    o = o0 - lam_full * o1
    ms = jnp.mean(o * o, axis=-1, keepdims=True)
    return (o * lax.rsqrt(ms + EPS) * subg) * (1.0 - _lambda_init(l))


def _attn_ctx_kernel(q_ref, k_ref, v_ref, lam_ref, subg_ref, o_ref, *, l):
    lam_full = _lambda_full(lam_ref, l)
    ones = jnp.ones((k_ref.shape[0], 2 * DV - DV), BF16)
    for h in range(N_HEADS):
        cols = slice(h * DV, (h + 1) * DV)
        vaug = jnp.concatenate([v_ref[:, cols], ones], axis=1)
        o_ref[:, cols] = _attend_head(q_ref[:, cols], k_ref[:, cols], vaug, lam_full, subg_ref[...], l).astype(BF16)


def _attn_ctx_call(q, k, v, lam, subln_g, *, l, seq):
    t = q.shape[0]
    blk = pl.BlockSpec((seq, BRANCH_WIDTH), lambda b: (b, 0))
    return pl.pallas_call(
        functools.partial(_attn_ctx_kernel, l=l),
        grid=(t // seq,),
        in_specs=[blk, blk, blk,
                  pl.BlockSpec((None, 4, DQK), lambda b: (l, 0, 0)),
                  pl.BlockSpec((None, 1, DV), lambda b: (l, 0, 0))],
        out_specs=blk,
        out_shape=jax.ShapeDtypeStruct((t, BRANCH_WIDTH), BF16),
        compiler_params=_params(("parallel",)),
        name="attn_ctx",
    )(q, k, v, lam, subln_g)


def _attn_lat_kernel(q_ref, k_ref, v_ref, ck_ref, cv_ref, lam_ref, subg_ref, o_ref, kall_ref, vaug_ref, *, l, past):
    qi = pl.program_id(1)

    @pl.when(qi == 0)
    def _():
        kall_ref[:past] = ck_ref[...].astype(BF16)
        kall_ref[past:] = k_ref[...]
        for h in range(N_HEADS):
            cols = slice(h * DV, (h + 1) * DV)
            vaug_ref[h, :past, :DV] = cv_ref[:, cols].astype(BF16)
            vaug_ref[h, past:, :DV] = v_ref[:, cols]
            vaug_ref[h, :, DV:] = jnp.ones((vaug_ref.shape[1], DV), BF16)

    lam_full = _lambda_full(lam_ref, l)
    for h in range(N_HEADS):
        cols = slice(h * DV, (h + 1) * DV)
        o_ref[:, cols] = _attend_head(q_ref[:, cols], kall_ref[:, cols], vaug_ref[h], lam_full, subg_ref[...], l).astype(BF16)


def _attn_lat_call(q, k, v, cache_k, cache_v, lam, subln_g, *, l, seq, tq):
    t = q.shape[0]
    past = cache_k.shape[2]
    nq = seq // tq
    bw = BRANCH_WIDTH
    kv = pl.BlockSpec((seq, bw), lambda b, i: (b, 0))
    cache = pl.BlockSpec((None, None, past, bw), lambda b, i: (b, l, 0, 0))
    return pl.pallas_call(
        functools.partial(_attn_lat_kernel, l=l, past=past),
        grid=(t // seq, nq),
        in_specs=[pl.BlockSpec((tq, bw), lambda b, i: (b * nq + i, 0)), kv, kv, cache, cache,
                  pl.BlockSpec((None, 4, DQK), lambda b, i: (l, 0, 0)),
                  pl.BlockSpec((None, 1, DV), lambda b, i: (l, 0, 0))],
        out_specs=pl.BlockSpec((tq, bw), lambda b, i: (b * nq + i, 0)),
        out_shape=jax.ShapeDtypeStruct((t, bw), BF16),
        scratch_shapes=[pltpu.VMEM((past + seq, bw), BF16), pltpu.VMEM((N_HEADS, past + seq, 2 * DV), BF16)],
        compiler_params=_params(("parallel", "arbitrary")),
        name="attn_lat",
    )(q, k, v, cache_k, cache_v, lam, subln_g)


def _mix_out_kernel(x_ref, gt_ref, p_ref, pprev_ref, pnext_ref, gb_ref, wc_ref, yb_ref, yc_ref,
                    ga_ref, gbb_ref, gc_ref, wa_ref, wb_ref, wcc_ref, wo_ref, o_ref, ya_ref, *, tm, seq, nn):
    i = pl.program_id(0)
    n = pl.program_id(1)

    @pl.when(n == 0)
    def _():
        p = p_ref[...].astype(F32)
        row = lax.broadcasted_iota(jnp.int32, p.shape, 0)
        pos = (i * tm + row) % seq
        halo_prev = pprev_ref[...].astype(F32)[HALO - 1:HALO, :]
        halo_next = pnext_ref[...].astype(F32)[0:1, :]
        prev = jnp.where(row == 0, halo_prev, pltpu.roll(p, 1, axis=0))
        prev = jnp.where(pos == 0, 0.0, prev)
        nxt = jnp.where(row == tm - 1, halo_next, pltpu.roll(p, tm - 1, axis=0))
        nxt = jnp.where(pos == seq - 1, 0.0, nxt)
        conv = wc_ref[0:1, :] * prev + wc_ref[1:2, :] * p + wc_ref[2:3, :] * nxt
        ya_ref[...] = (gb_ref[...].astype(F32) * conv).astype(BF16)

    ya = jnp.dot(ya_ref[...], wa_ref[...], preferred_element_type=F32)
    yb = jnp.dot(yb_ref[...], wb_ref[...], preferred_element_type=F32)
    yc = jnp.dot(yc_ref[...], wcc_ref[...], preferred_element_type=F32)
    merged = (ga_ref[...].astype(F32) * ya + gbb_ref[...].astype(F32) * yb + gc_ref[...].astype(F32) * yc)
    part = jnp.dot(merged.astype(BF16), wo_ref[...], preferred_element_type=F32)

    @pl.when(n == 0)
    def _():
        o_ref[...] = part

    @pl.when(n > 0)
    def _():
        o_ref[...] += part

    @pl.when(n == nn - 1)
    def _():
        o_ref[...] = x_ref[...] + gt_ref[...] * o_ref[...]


def _mix_out_call(x, mod, p, gb, w_conv, yb, yc, gates, w_branch, w_o, *, l, tm, tn, seq, row_of_tile):
    t = x.shape[0]
    bw = BRANCH_WIDTH
    nn = D_MODEL // tn
    halo = HALO
    last_halo = t // halo - 1
    act = pl.BlockSpec((tm, bw), lambda i, n: (i, 0))
    return pl.pallas_call(
        functools.partial(_mix_out_kernel, tm=tm, seq=seq, nn=nn),
        grid=(t // tm, nn),
        in_specs=[
            pl.BlockSpec((tm, D_MODEL), lambda i, n: (i, 0)),
            _mod_spec(l, 5, row_of_tile),
            act,
            pl.BlockSpec((halo, bw), lambda i, n: (jnp.maximum(i * (tm // halo) - 1, 0), 0)),
            pl.BlockSpec((halo, bw), lambda i, n: (jnp.minimum((i + 1) * (tm // halo), last_halo), 0)),
            act,
            pl.BlockSpec((None, 3, bw), lambda i, n: (l, 0, 0)),
            act,
            act,
            pl.BlockSpec((tm, tn), lambda i, n: (i, n)),
            pl.BlockSpec((tm, tn), lambda i, n: (i, nn + n)),
            pl.BlockSpec((tm, tn), lambda i, n: (i, 2 * nn + n)),
            pl.BlockSpec((None, None, bw, tn), lambda i, n: (l, 0, 0, n)),
            pl.BlockSpec((None, None, bw, tn), lambda i, n: (l, 1, 0, n)),
            pl.BlockSpec((None, None, bw, tn), lambda i, n: (l, 2, 0, n)),
            pl.BlockSpec((None, tn, D_MODEL), lambda i, n: (l, n, 0)),
        ],
        out_specs=pl.BlockSpec((tm, D_MODEL), lambda i, n: (i, 0)),
        out_shape=jax.ShapeDtypeStruct((t, D_MODEL), F32),
        scratch_shapes=[pltpu.VMEM((tm, bw), BF16)],
        compiler_params=_params(("parallel", "arbitrary")),
        name="mix_out",
    )(x, mod, p, p, p, gb, w_conv, yb, yc, gates, gates, gates, w_branch, w_branch, w_branch, w_o)


def _rope_tables(seq):
    rows = seq // GRID_W
    row = jnp.repeat(jnp.arange(rows, dtype=F32), GRID_W)
    col = jnp.tile(jnp.arange(GRID_W, dtype=F32), rows)
    inv = ROPE_THETA ** (-jnp.arange(0, AXIS_DIM, 2, dtype=F32) / AXIS_DIM)
    ang_r, ang_c = row[:, None] * inv, col[:, None] * inv
    cos = jnp.concatenate([jnp.cos(ang_r)] * 2 + [jnp.cos(ang_c)] * 2, axis=1)
    sin = jnp.concatenate([-jnp.sin(ang_r), jnp.sin(ang_r), -jnp.sin(ang_c), jnp.sin(ang_c)], axis=1)
    return jnp.tile(cos, (1, LANES // DQK)), jnp.tile(sin, (1, LANES // DQK))


def kernel(x_prompt, x_sample, c, c_ctx, cache_k, cache_v, w_mod, b_mod, norm_g, w_ffn_in, w_ffn_out,
           w_in, w_conv, qk_gain, lam, subln_g, gmlp_ln_g, gmlp_ln_b, w_s, b_s, w_branch, w_o):
    batch, seq, _ = x_prompt.shape
    dec_batch, dec_seq, _ = x_sample.shape
    depth = w_mod.shape[0]
    past = cache_k.shape[2]
    bw = BRANCH_WIDTH
    assert dec_batch + 1 <= MOD_ROWS and seq % CHUNK == 0 and dec_seq % CHUNK == 0

    tm_ctx = 2 * seq if batch % 2 == 0 else seq
    tm_lat = 512
    tf = 512
    tn_out = 512
    tq = 256

    cvec = jnp.zeros((MOD_ROWS, D_MODEL), F32).at[0].set(c_ctx).at[1:1 + dec_batch].set(c)
    mod = _mod_call(cvec, w_mod, b_mod).reshape(depth, MOD_ROWS, N_MOD, 1, D_MODEL)

    w_ffn_in_b = w_ffn_in.astype(BF16)
    w_ffn_out_b = w_ffn_out.astype(BF16)
    w_in_b = w_in.astype(BF16)
    w_branch_b = w_branch.astype(BF16)
    w_o_b = w_o.astype(BF16)
    w_s_b = w_s.astype(BF16)
    norm_g4 = norm_g.reshape(depth, 3, 1, D_MODEL)
    qk_gain_t = jnp.tile(qk_gain, (1, 1, bw // DQK)).reshape(depth, 2, 1, bw)
    ln_g3 = gmlp_ln_g.reshape(depth, 1, bw)
    ln_b3 = gmlp_ln_b.reshape(depth, 1, bw)
    b_s_t = jnp.repeat(jnp.swapaxes(b_s, 1, 2), bw // GMLP_GROUPS, axis=2)
    subg3 = subln_g.reshape(depth, 1, DV)
    rope = _rope_tables(dec_seq)
    cache_k4 = cache_k.reshape(dec_batch, depth, past, bw)
    cache_v4 = cache_v.reshape(dec_batch, depth, past, bw)

    ctx_row = lambda i: 0
    lat_tiles_per_seq = dec_seq // tm_lat
    lat_row = lambda i: 1 + i // lat_tiles_per_seq

    xp = x_prompt.reshape(batch * seq, D_MODEL)
    xs = x_sample.reshape(dec_batch * dec_seq, D_MODEL)
    new_k = jnp.zeros((batch, depth, seq, bw), F32)
    new_v = jnp.zeros((batch, depth, seq, bw), F32)

    for l in range(depth):
        for path in ("ctx", "lat"):
            is_ctx = path == "ctx"
            x = xp if is_ctx else xs
            tm = tm_ctx if is_ctx else tm_lat
            sq = seq if is_ctx else dec_seq
            row_of_tile = ctx_row if is_ctx else lat_row
            x = _ffn_call(x, mod, norm_g4, w_ffn_in_b, w_ffn_out_b, l=l, s=0, tm=tm, tf=tf, row_of_tile=row_of_tile)
            outs = _mix_in_call(x, mod, norm_g4, w_in_b, qk_gain_t, None if is_ctx else rope, ln_g3, ln_b3, w_s_b, b_s_t,
                                (new_k, new_v) if is_ctx else None, l=l, tm=tm, seq=sq, row_of_tile=row_of_tile)
            p, gb, q, k, v, yc, gates = outs[:7]
            if is_ctx:
                new_k, new_v = outs[7], outs[8]
                yb = _attn_ctx_call(q, k, v, lam, subg3, l=l, seq=sq)
            else:
                yb = _attn_lat_call(q, k, v, cache_k4, cache_v4, lam, subg3, l=l, seq=sq, tq=tq)
            x = _mix_out_call(x, mod, p, gb, w_conv, yb, yc, gates, w_branch_b, w_o_b,
                              l=l, tm=tm, tn=tn_out, seq=sq, row_of_tile=row_of_tile)
            x = _ffn_call(x, mod, norm_g4, w_ffn_in_b, w_ffn_out_b, l=l, s=1, tm=tm, tf=tf, row_of_tile=row_of_tile)
            if is_ctx:
                xp = x
            else:
                xs = x

    return (xp.reshape(batch, seq, D_MODEL), xs.reshape(dec_batch, dec_seq, D_MODEL),
            new_k.reshape(batch, depth, seq, N_HEADS, 2, DQK), new_v.reshape(batch, depth, seq, N_HEADS, DV))
```

```python
import functools
import math

import jax
import jax.numpy as jnp
from jax import lax
from jax.experimental import pallas as pl
from jax.experimental.pallas import tpu as pltpu

D_MODEL = 2048
GRID_W = 64
N_MOD = 9
D_FF = 5632
BRANCH_WIDTH = D_MODEL // 2
N_HEADS = 8
DQK = 64
DV = 2 * DQK
GMLP_GROUPS = 8
CHUNK = 128
N_BRANCH = 3
ROPE_THETA = 10000.0
AXIS_DIM = DQK // 2
EPS = 1e-6
N_IN_GROUPS = 8 + 2 * N_BRANCH
LANES = 128
MXU_DIM = 256
MOD_ROWS = 16
HALO = 16
VMEM_LIMIT = 56 * 1024 * 1024

G_XA, G_GB, G_GC, G_Q, G_K, G_V, G_U, G_VG, G_GATES = range(9)
SLOT_XA, SLOT_GB, SLOT_GC, SLOT_V, SLOT_GATES = 0, 1, 2, 3, 4
N_SLOTS = SLOT_GATES + 2 * N_BRANCH

F32 = jnp.float32
BF16 = jnp.bfloat16


def _slab_slot(j):
    return jnp.where(j < G_Q, j, jnp.where(j <= G_V, SLOT_V, jnp.where(j <= G_GATES, SLOT_GATES, j - G_GATES + SLOT_GATES)))


def _params(semantics):
    return pltpu.CompilerParams(dimension_semantics=semantics, vmem_limit_bytes=VMEM_LIMIT)


def _sigmoid(t):
    return 0.5 * jnp.tanh(0.5 * t) + 0.5


def _modulated_norm(x, gain, shift, scale):
    ms = jnp.mean(x * x, axis=-1, keepdims=True)
    return (x * lax.rsqrt(ms + EPS)) * (gain * (1.0 + scale)) + shift


def _mod_kernel(c_ref, w_ref, b_ref, o_ref):
    c = c_ref[...]
    a = (c * _sigmoid(c)).astype(BF16)
    o_ref[...] = jnp.dot(a, w_ref[...].astype(BF16), preferred_element_type=F32) + b_ref[...]


def _mod_call(cvec, w_mod, b_mod):
    depth = w_mod.shape[0]
    tn = 1024
    nn = N_MOD * D_MODEL // tn
    return pl.pallas_call(
        _mod_kernel,
        grid=(depth, nn),
        in_specs=[
            pl.BlockSpec((MOD_ROWS, D_MODEL), lambda l, n: (0, 0)),
            pl.BlockSpec((None, D_MODEL, tn), lambda l, n: (l, 0, n)),
            pl.BlockSpec((None, 1, tn), lambda l, n: (l, 0, n)),
        ],
        out_specs=pl.BlockSpec((None, MOD_ROWS, tn), lambda l, n: (l, 0, n)),
        out_shape=jax.ShapeDtypeStruct((depth, MOD_ROWS, N_MOD * D_MODEL), F32),
        compiler_params=_params(("parallel", "parallel")),
        name="adaln_mod",
    )(cvec, w_mod, b_mod.reshape(depth, 1, N_MOD * D_MODEL))


def _mod_spec(l, j, row_of_tile):
    return pl.BlockSpec((None, None, None, 1, D_MODEL), lambda i, n: (l, row_of_tile(i), j, 0, 0))


def _ffn_kernel(x_ref, g_ref, sh_ref, sc_ref, gt_ref, wa_ref, wg_ref, wo_ref, o_ref, h_ref):
    @pl.when(pl.program_id(1) == 0)
    def _():
        x = x_ref[...]
        h_ref[...] = _modulated_norm(x, g_ref[...], sh_ref[...], sc_ref[...]).astype(BF16)
        o_ref[...] = x

    h = h_ref[...]
    a = jnp.dot(h, wa_ref[...], preferred_element_type=F32)
    g = jnp.dot(h, wg_ref[...], preferred_element_type=F32)
    act = (a * _sigmoid(a) * g).astype(BF16)
    o_ref[...] += (0.5 * gt_ref[...]) * jnp.dot(act, wo_ref[...], preferred_element_type=F32)


def _ffn_call(x, mod, norm_g, w_in, w_out, *, l, s, tm, tf, row_of_tile):
    t = x.shape[0]
    nf = D_FF // tf
    mod_base = 0 if s == 0 else 6
    norm_idx = 0 if s == 0 else 2
    return pl.pallas_call(
        _ffn_kernel,
        grid=(t // tm, nf),
        in_specs=[
            pl.BlockSpec((tm, D_MODEL), lambda i, f: (i, 0)),
            pl.BlockSpec((None, None, 1, D_MODEL), lambda i, f: (l, norm_idx, 0, 0)),
            _mod_spec(l, mod_base + 0, row_of_tile),
            _mod_spec(l, mod_base + 1, row_of_tile),
            _mod_spec(l, mod_base + 2, row_of_tile),
            pl.BlockSpec((None, None, D_MODEL, tf), lambda i, f: (l, s, 0, f)),
            pl.BlockSpec((None, None, D_MODEL, tf), lambda i, f: (l, s, 0, nf + f)),
            pl.BlockSpec((None, None, tf, D_MODEL), lambda i, f: (l, s, f, 0)),
        ],
        out_specs=pl.BlockSpec((tm, D_MODEL), lambda i, f: (i, 0)),
        out_shape=jax.ShapeDtypeStruct((t, D_MODEL), F32),
        scratch_shapes=[pltpu.VMEM((tm, D_MODEL), BF16)],
        compiler_params=_params(("parallel", "arbitrary")),
        name=f"ffn{s}",
    )(x, norm_g, mod, mod, mod, w_in, w_in, w_out)


def _partner(y):
    lane = lax.broadcasted_iota(jnp.int32, y.shape, 1)
    ahead = pltpu.roll(y, LANES - AXIS_DIM // 2, axis=1)
    behind = pltpu.roll(y, AXIS_DIM // 2, axis=1)
    return jnp.where(lane % AXIS_DIM < AXIS_DIM // 2, ahead, behind)


def _segment_ones():
    r = lax.broadcasted_iota(jnp.int32, (MXU_DIM, MXU_DIM), 0)
    c = lax.broadcasted_iota(jnp.int32, (MXU_DIM, MXU_DIM), 1)
    return jnp.where(r // DQK == c // DQK, 1.0, 0.0).astype(BF16)


def _qk_norm(z, cos_g, sin_g, store):
    seg = _segment_ones()
    for pair in range(BRANCH_WIDTH // MXU_DIM):
        zp = z[:, pair * MXU_DIM:(pair + 1) * MXU_DIM]
        ss = jnp.dot((zp * zp).astype(BF16), seg, preferred_element_type=F32)
        r = lax.rsqrt(ss * (1.0 / DQK) + EPS)
        for half in range(MXU_DIM // LANES):
            zc = zp[:, half * LANES:(half + 1) * LANES]
            y = zc * cos_g
            if sin_g is not None:
                y = y + _partner(zc) * sin_g
            store(pair * (MXU_DIM // LANES) + half, y * r[:, half * LANES:(half + 1) * LANES])


def _mix_in_kernel(*refs, tm, seq, is_ctx):
    it = iter(refs)
    x_ref, g_ref, sh_ref, sc_ref, w_ref, qkg_ref = (next(it) for _ in range(6))
    cos_ref = sin_ref = None
    if not is_ctx:
        cos_ref, sin_ref = next(it), next(it)
    lng_ref, lnb_ref, ws_ref, bs_ref = (next(it) for _ in range(4))
    if is_ctx:
        next(it), next(it)
    slab_ref, q_ref, k_ref, yc_ref = (next(it) for _ in range(4))
    kc_ref = vc_ref = None
    if is_ctx:
        kc_ref, vc_ref = next(it), next(it)
    h_ref, ug_ref = next(it), next(it)
    tab_ref = None if is_ctx else next(it)

    j = pl.program_id(1)
    q_scale = DQK ** -0.5

    def project():
        return jnp.dot(h_ref[...], w_ref[...], preferred_element_type=F32)

    @pl.when(j == G_XA)
    def _():
        h = _modulated_norm(x_ref[...], g_ref[...], sh_ref[...], sc_ref[...]).astype(BF16)
        h_ref[...] = h
        if not is_ctx:
            cos, sin = cos_ref[...], sin_ref[...]
            gq = jnp.broadcast_to(qkg_ref[0, :, :LANES], cos.shape) * q_scale
            gk = jnp.broadcast_to(qkg_ref[1, :, :LANES], cos.shape)
            tab_ref[0] = cos * gq
            tab_ref[1] = sin * _partner(gq)
            tab_ref[2] = cos * gk
            tab_ref[3] = sin * _partner(gk)
        slab_ref[...] = jnp.dot(h, w_ref[...], preferred_element_type=F32).astype(BF16)

    cast_only = (j == G_GB) | (j == G_GC) | (j >= G_GATES)
    if not is_ctx:
        cast_only = cast_only | (j == G_V)

    @pl.when(cast_only)
    def _():
        slab_ref[...] = project().astype(BF16)

    if is_ctx:
        @pl.when(j == G_V)
        def _():
            z = project()
            slab_ref[...] = z.astype(BF16)
            for b in range(tm // seq):
                vc_ref[b] = z[b * seq:(b + 1) * seq]

    def store_q(c, y):
        q_ref[:, c * LANES:(c + 1) * LANES] = y.astype(BF16)

    def store_k(c, y):
        k_ref[:, c * LANES:(c + 1) * LANES] = y.astype(BF16)
        if is_ctx:
            for b in range(tm // seq):
                kc_ref[b, :, c * LANES:(c + 1) * LANES] = y[b * seq:(b + 1) * seq]

    @pl.when(j == G_Q)
    def _():
        if is_ctx:
            _qk_norm(project(), qkg_ref[0, :, :LANES] * q_scale, None, store_q)
        else:
            _qk_norm(project(), tab_ref[0], tab_ref[1], store_q)

    @pl.when(j == G_K)
    def _():
        if is_ctx:
            _qk_norm(project(), qkg_ref[1, :, :LANES], None, store_k)
        else:
            _qk_norm(project(), tab_ref[2], tab_ref[3], store_k)

    @pl.when(j == G_U)
    def _():
        ug_ref[...] = jax.nn.gelu(project())

    @pl.when(j == G_VG)
    def _():
        gv = jax.nn.gelu(project())
        mu = jnp.mean(gv, axis=-1, keepdims=True)
        d = gv - mu
        var = jnp.mean(d * d, axis=-1, keepdims=True)
        vn = (d * lax.rsqrt(var + EPS) * lng_ref[...] + lnb_ref[...]).astype(BF16)
        nch = tm // CHUNK
        for g in range(GMLP_GROUPS):
            cols = slice(g * CHUNK, (g + 1) * CHUNK)
            stacked = jnp.concatenate([vn[n * CHUNK:(n + 1) * CHUNK, cols] for n in range(nch)], axis=1)
            mixed = jnp.dot(ws_ref[g], stacked, preferred_element_type=F32)
            for n in range(nch):
                rows = slice(n * CHUNK, (n + 1) * CHUNK)
                m = mixed[:, n * CHUNK:(n + 1) * CHUNK] + bs_ref[:, cols]
                yc_ref[rows, cols] = (ug_ref[rows, cols] * m).astype(BF16)


def _mix_in_call(x, mod, norm_g, w_in, qk_gain_t, rope, ln_g, ln_b, w_s, b_s_t, caches, *, l, tm, seq, row_of_tile):
    t = x.shape[0]
    is_ctx = caches is not None
    bw = BRANCH_WIDTH
    tok = lambda i, j: (i, 0)
    in_specs = [
        pl.BlockSpec((tm, D_MODEL), tok),
        pl.BlockSpec((None, None, 1, D_MODEL), lambda i, j: (l, 1, 0, 0)),
        _mod_spec(l, 3, row_of_tile),
        _mod_spec(l, 4, row_of_tile),
        pl.BlockSpec((None, D_MODEL, bw), lambda i, j: (l, 0, j)),
        pl.BlockSpec((None, 2, 1, bw), lambda i, j: (l, 0, 0, 0)),
    ]
    args = [x, norm_g, mod, mod, w_in, qk_gain_t]
    scratch = [pltpu.VMEM((tm, D_MODEL), BF16), pltpu.VMEM((tm, bw), F32)]
    if not is_ctx:
        tiles_per_seq = seq // tm
        in_specs += [pl.BlockSpec((tm, LANES), lambda i, j: (i % tiles_per_seq, 0))] * 2
        args += [rope[0], rope[1]]
        scratch += [pltpu.VMEM((4, tm, LANES), F32)]
    in_specs += [
        pl.BlockSpec((None, 1, bw), lambda i, j: (l, 0, 0)),
        pl.BlockSpec((None, 1, bw), lambda i, j: (l, 0, 0)),
        pl.BlockSpec((None, GMLP_GROUPS, CHUNK, CHUNK), lambda i, j: (l, 0, 0, 0)),
        pl.BlockSpec((None, CHUNK, bw), lambda i, j: (l, 0, 0)),
    ]
    args += [ln_g, ln_b, w_s, b_s_t]
    act = jax.ShapeDtypeStruct((t, bw), BF16)
    out_shape = [jax.ShapeDtypeStruct((t, N_SLOTS * bw), BF16), act, act, act]
    out_specs = [pl.BlockSpec((tm, bw), lambda i, j: (i, _slab_slot(j)))] + [pl.BlockSpec((tm, bw), tok)] * 3
    aliases = {}
    if is_ctx:
        nb = tm // seq
        cache_spec = pl.BlockSpec((nb, None, seq, bw), lambda i, j: (i, l, 0, 0))
        in_specs += [pl.BlockSpec(memory_space=pl.ANY)] * 2
        aliases = {len(args): len(out_shape), len(args) + 1: len(out_shape) + 1}
        args += list(caches)
        out_shape += [jax.ShapeDtypeStruct(c.shape, c.dtype) for c in caches]
        out_specs += [cache_spec, cache_spec]
    return pl.pallas_call(
        functools.partial(_mix_in_kernel, tm=tm, seq=seq, is_ctx=is_ctx),
        grid=(t // tm, N_IN_GROUPS),
        in_specs=in_specs,
        out_specs=out_specs,
        out_shape=out_shape,
        input_output_aliases=aliases,
        scratch_shapes=scratch,
        compiler_params=_params(("parallel", "arbitrary")),
        name="mix_in_ctx" if is_ctx else "mix_in_lat",
    )(*args)


def _lambda_init(l):
    return 0.8 - 0.6 * math.exp(-0.3 * l)


def _lambda_full(lam_ref, l):
    lp = lam_ref[...]
    a = jnp.sum(lp[0:1] * lp[1:2], axis=-1, keepdims=True)
    b = jnp.sum(lp[2:3] * lp[3:4], axis=-1, keepdims=True)
    return jnp.exp(a) - jnp.exp(b) + _lambda_init(l)


def _softmax_times_values(qm, k, vaug):
    s = lax.dot_general(qm, k, (((1,), (1,)), ((), ())), preferred_element_type=F32)
    e = jnp.exp(s - jnp.max(s, axis=-1, keepdims=True)).astype(BF16)
    pv = jnp.dot(e, vaug, preferred_element_type=F32)
    return pv[:, :DV] / pv[:, DV:]


def _attend_head(qh, kh, vaug, lam_full, subg, l):
    lane = lax.broadcasted_iota(jnp.int32, qh.shape, 1)
    zero = jnp.zeros_like(qh)
    o0 = _softmax_times_values(jnp.where(lane < DQK, qh, zero), kh, vaug)
    o1 = _softmax_times_values(jnp.where(lane < DQK, zero, qh), kh, vaug)
    o = o0 - lam_full * o1
    ms = jnp.mean(o * o, axis=-1, keepdims=True)
    return (o * lax.rsqrt(ms + EPS) * subg) * (1.0 - _lambda_init(l))


def _attn_ctx_kernel(q_ref, k_ref, v_ref, lam_ref, subg_ref, o_ref, *, l):
    lam_full = _lambda_full(lam_ref, l)
    ones = jnp.ones((k_ref.shape[0], DV), BF16)
    for h in range(N_HEADS):
        cols = slice(h * DV, (h + 1) * DV)
        vaug = jnp.concatenate([v_ref[:, cols], ones], axis=1)
        o_ref[:, cols] = _attend_head(q_ref[:, cols], k_ref[:, cols], vaug, lam_full, subg_ref[...], l).astype(BF16)


def _attn_ctx_call(q, k, slab, lam, subln_g, *, l, seq):
    t = q.shape[0]
    blk = pl.BlockSpec((seq, BRANCH_WIDTH), lambda b: (b, 0))
    return pl.pallas_call(
        functools.partial(_attn_ctx_kernel, l=l),
        grid=(t // seq,),
        in_specs=[blk, blk, pl.BlockSpec((seq, BRANCH_WIDTH), lambda b: (b, SLOT_V)),
                  pl.BlockSpec((None, 4, DQK), lambda b: (l, 0, 0)),
                  pl.BlockSpec((None, 1, DV), lambda b: (l, 0, 0))],
        out_specs=blk,
        out_shape=jax.ShapeDtypeStruct((t, BRANCH_WIDTH), BF16),
        compiler_params=_params(("parallel",)),
        name="attn_ctx",
    )(q, k, slab, lam, subln_g)


def _attn_lat_kernel(q_ref, k_ref, v_ref, ck_ref, cv_ref, lam_ref, subg_ref, o_ref, kall_ref, vaug_ref, *, l, past):
    @pl.when(pl.program_id(1) == 0)
    def _():
        kall_ref[:past] = ck_ref[...].astype(BF16)
        kall_ref[past:] = k_ref[...]
        for h in range(N_HEADS):
            cols = slice(h * DV, (h + 1) * DV)
            vaug_ref[h, :past, :DV] = cv_ref[:, cols].astype(BF16)
            vaug_ref[h, past:, :DV] = v_ref[:, cols]
            vaug_ref[h, :, DV:] = jnp.ones((vaug_ref.shape[1], DV), BF16)

    lam_full = _lambda_full(lam_ref, l)
    for h in range(N_HEADS):
        cols = slice(h * DV, (h + 1) * DV)
        o_ref[:, cols] = _attend_head(q_ref[:, cols], kall_ref[:, cols], vaug_ref[h], lam_full, subg_ref[...], l).astype(BF16)


def _attn_lat_call(q, k, slab, cache_k, cache_v, lam, subln_g, *, l, seq, tq):
    t = q.shape[0]
    past = cache_k.shape[2]
    nq = seq // tq
    bw = BRANCH_WIDTH
    cache = pl.BlockSpec((None, None, past, bw), lambda b, i: (b, l, 0, 0))
    return pl.pallas_call(
        functools.partial(_attn_lat_kernel, l=l, past=past),
        grid=(t // seq, nq),
        in_specs=[pl.BlockSpec((tq, bw), lambda b, i: (b * nq + i, 0)),
                  pl.BlockSpec((seq, bw), lambda b, i: (b, 0)),
                  pl.BlockSpec((seq, bw), lambda b, i: (b, SLOT_V)),
                  cache, cache,
                  pl.BlockSpec((None, 4, DQK), lambda b, i: (l, 0, 0)),
                  pl.BlockSpec((None, 1, DV), lambda b, i: (l, 0, 0))],
        out_specs=pl.BlockSpec((tq, bw), lambda b, i: (b * nq + i, 0)),
        out_shape=jax.ShapeDtypeStruct((t, bw), BF16),
        scratch_shapes=[pltpu.VMEM((past + seq, bw), BF16), pltpu.VMEM((N_HEADS, past + seq, 2 * DV), BF16)],
        compiler_params=_params(("parallel", "arbitrary")),
        name="attn_lat",
    )(q, k, slab, cache_k, cache_v, lam, subln_g)


def _mix_out_kernel(x_ref, gt_ref, xa_ref, xa_prev_ref, xa_next_ref, gc_ref, gc_prev_ref, gc_next_ref, gb_ref, wc_ref,
                    yb_ref, yc_ref, za_ref, zb_ref, zc_ref, wa_ref, wb_ref, wcc_ref, wo_ref, o_ref, ya_ref, *, tm, seq):
    @pl.when(pl.program_id(1) == 0)
    def _():
        p = xa_ref[...].astype(F32) * gc_ref[...].astype(F32)
        row = lax.broadcasted_iota(jnp.int32, p.shape, 0)
        pos = (pl.program_id(0) * tm + row) % seq
        halo_prev = (xa_prev_ref[...].astype(F32) * gc_prev_ref[...].astype(F32))[HALO - 1:HALO, :]
        halo_next = (xa_next_ref[...].astype(F32) * gc_next_ref[...].astype(F32))[0:1, :]
        prev = jnp.where(row == 0, halo_prev, pltpu.roll(p, 1, axis=0))
        prev = jnp.where(pos == 0, 0.0, prev)
        nxt = jnp.where(row == tm - 1, halo_next, pltpu.roll(p, tm - 1, axis=0))
        nxt = jnp.where(pos == seq - 1, 0.0, nxt)
        conv = wc_ref[0:1, :] * prev + wc_ref[1:2, :] * p + wc_ref[2:3, :] * nxt
        ya_ref[...] = (gb_ref[...].astype(F32) * conv).astype(BF16)
        o_ref[...] = x_ref[...]

    ya = jnp.dot(ya_ref[...], wa_ref[...], preferred_element_type=F32)
    yb = jnp.dot(yb_ref[...], wb_ref[...], preferred_element_type=F32)
    yc = jnp.dot(yc_ref[...], wcc_ref[...], preferred_element_type=F32)
    merged = (_sigmoid(za_ref[...].astype(F32)) * ya + _sigmoid(zb_ref[...].astype(F32)) * yb
              + _sigmoid(zc_ref[...].astype(F32)) * yc)
    o_ref[...] += gt_ref[...] * jnp.dot(merged.astype(BF16), wo_ref[...], preferred_element_type=F32)


def _mix_out_call(x, mod, slab, w_conv, yb, yc, w_branch, w_o, *, l, tm, tn, seq, row_of_tile):
    t = x.shape[0]
    bw = BRANCH_WIDTH
    nn = D_MODEL // tn
    last_halo = t // HALO - 1
    gate0 = SLOT_GATES * bw // tn

    def main(slot):
        return pl.BlockSpec((tm, bw), lambda i, n: (i, slot))

    def prev(slot):
        return pl.BlockSpec((HALO, bw), lambda i, n: (jnp.maximum(i * (tm // HALO) - 1, 0), slot))

    def nxt(slot):
        return pl.BlockSpec((HALO, bw), lambda i, n: (jnp.minimum((i + 1) * (tm // HALO), last_halo), slot))

    def gate(branch):
        return pl.BlockSpec((tm, tn), lambda i, n: (i, gate0 + branch * nn + n))

    def branch_w(branch):
        return pl.BlockSpec((None, None, bw, tn), lambda i, n: (l, branch, 0, n))

    act = pl.BlockSpec((tm, bw), lambda i, n: (i, 0))
    return pl.pallas_call(
        functools.partial(_mix_out_kernel, tm=tm, seq=seq),
        grid=(t // tm, nn),
        in_specs=[
            pl.BlockSpec((tm, D_MODEL), lambda i, n: (i, 0)),
            _mod_spec(l, 5, row_of_tile),
            main(SLOT_XA), prev(SLOT_XA), nxt(SLOT_XA),
            main(SLOT_GC), prev(SLOT_GC), nxt(SLOT_GC),
            main(SLOT_GB),
            pl.BlockSpec((None, 3, bw), lambda i, n: (l, 0, 0)),
            act, act,
            gate(0), gate(1), gate(2),
            branch_w(0), branch_w(1), branch_w(2),
            pl.BlockSpec((None, tn, D_MODEL), lambda i, n: (l, n, 0)),
        ],
        out_specs=pl.BlockSpec((tm, D_MODEL), lambda i, n: (i, 0)),
        out_shape=jax.ShapeDtypeStruct((t, D_MODEL), F32),
        scratch_shapes=[pltpu.VMEM((tm, bw), BF16)],
        compiler_params=_params(("parallel", "arbitrary")),
        name="mix_out",
    )(x, mod, slab, slab, slab, slab, slab, slab, slab, w_conv, yb, yc, slab, slab, slab,
      w_branch, w_branch, w_branch, w_o)


def _rope_tables(seq):
    rows = seq // GRID_W
    row = jnp.repeat(jnp.arange(rows, dtype=F32), GRID_W)
    col = jnp.tile(jnp.arange(GRID_W, dtype=F32), rows)
    inv = ROPE_THETA ** (-jnp.arange(0, AXIS_DIM, 2, dtype=F32) / AXIS_DIM)
    ang_r, ang_c = row[:, None] * inv, col[:, None] * inv
    cos = jnp.concatenate([jnp.cos(ang_r)] * 2 + [jnp.cos(ang_c)] * 2, axis=1)
    sin = jnp.concatenate([-jnp.sin(ang_r), jnp.sin(ang_r), -jnp.sin(ang_c), jnp.sin(ang_c)], axis=1)
    return jnp.tile(cos, (1, LANES // DQK)), jnp.tile(sin, (1, LANES // DQK))


def kernel(x_prompt, x_sample, c, c_ctx, cache_k, cache_v, w_mod, b_mod, norm_g, w_ffn_in, w_ffn_out,
           w_in, w_conv, qk_gain, lam, subln_g, gmlp_ln_g, gmlp_ln_b, w_s, b_s, w_branch, w_o):
    batch, seq, _ = x_prompt.shape
    dec_batch, dec_seq, _ = x_sample.shape
    depth = w_mod.shape[0]
    past = cache_k.shape[2]
    bw = BRANCH_WIDTH
    assert dec_batch + 1 <= MOD_ROWS and seq % CHUNK == 0 and dec_seq % CHUNK == 0

    tm_ctx = 2 * seq if batch % 2 == 0 else seq
    tm_lat = 512
    tf = 512
    tn_out = 512
    tq = 256

    cvec = jnp.zeros((MOD_ROWS, D_MODEL), F32).at[0].set(c_ctx).at[1:1 + dec_batch].set(c)
    mod = _mod_call(cvec, w_mod, b_mod).reshape(depth, MOD_ROWS, N_MOD, 1, D_MODEL)

    w_ffn_in_b = w_ffn_in.astype(BF16)
    w_ffn_out_b = w_ffn_out.astype(BF16)
    w_in_b = w_in.astype(BF16)
    w_branch_b = w_branch.astype(BF16)
    w_o_b = w_o.astype(BF16)
    w_s_b = w_s.astype(BF16)
    norm_g4 = norm_g.reshape(depth, 3, 1, D_MODEL)
    qk_gain_t = jnp.tile(qk_gain, (1, 1, bw // DQK)).reshape(depth, 2, 1, bw)
    ln_g3 = gmlp_ln_g.reshape(depth, 1, bw)
    ln_b3 = gmlp_ln_b.reshape(depth, 1, bw)
    b_s_t = jnp.repeat(jnp.swapaxes(b_s, 1, 2), bw // GMLP_GROUPS, axis=2)
    subg3 = subln_g.reshape(depth, 1, DV)
    rope = _rope_tables(dec_seq)
    cache_k4 = cache_k.reshape(dec_batch, depth, past, bw)
    cache_v4 = cache_v.reshape(dec_batch, depth, past, bw)

    ctx_row = lambda i: 0
    lat_tiles_per_seq = dec_seq // tm_lat
    lat_row = lambda i: 1 + i // lat_tiles_per_seq

    xp = x_prompt.reshape(batch * seq, D_MODEL)
    xs = x_sample.reshape(dec_batch * dec_seq, D_MODEL)
    new_k = jnp.zeros((batch, depth, seq, bw), F32)
    new_v = jnp.zeros((batch, depth, seq, bw), F32)

    for l in range(depth):
        for path in ("ctx", "lat"):
            is_ctx = path == "ctx"
            x = xp if is_ctx else xs
            tm = tm_ctx if is_ctx else tm_lat
            sq = seq if is_ctx else dec_seq
            row_of_tile = ctx_row if is_ctx else lat_row
            x = _ffn_call(x, mod, norm_g4, w_ffn_in_b, w_ffn_out_b, l=l, s=0, tm=tm, tf=tf, row_of_tile=row_of_tile)
            outs = _mix_in_call(x, mod, norm_g4, w_in_b, qk_gain_t, None if is_ctx else rope, ln_g3, ln_b3, w_s_b, b_s_t,
                                (new_k, new_v) if is_ctx else None, l=l, tm=tm, seq=sq, row_of_tile=row_of_tile)
            slab, q, k, yc = outs[:4]
            if is_ctx:
                new_k, new_v = outs[4], outs[5]
                yb = _attn_ctx_call(q, k, slab, lam, subg3, l=l, seq=sq)
            else:
                yb = _attn_lat_call(q, k, slab, cache_k4, cache_v4, lam, subg3, l=l, seq=sq, tq=tq)
            x = _mix_out_call(x, mod, slab, w_conv, yb, yc, w_branch_b, w_o_b,
                              l=l, tm=tm, tn=tn_out, seq=sq, row_of_tile=row_of_tile)
            x = _ffn_call(x, mod, norm_g4, w_ffn_in_b, w_ffn_out_b, l=l, s=1, tm=tm, tf=tf, row_of_tile=row_of_tile)
            if is_ctx:
                xp = x
            else:
                xs = x

    return (xp.reshape(batch, seq, D_MODEL), xs.reshape(dec_batch, dec_seq, D_MODEL),
            new_k.reshape(batch, depth, seq, N_HEADS, 2, DQK), new_v.reshape(batch, depth, seq, N_HEADS, DV))
```

```python
import functools
import math

import jax
import jax.numpy as jnp
from jax import lax
from jax.experimental import pallas as pl
from jax.experimental.pallas import tpu as pltpu

D_MODEL = 2048
GRID_W = 64
N_MOD = 9
D_FF = 5632
BRANCH_WIDTH = D_MODEL // 2
N_HEADS = 8
DQK = 64
DV = 2 * DQK
GMLP_GROUPS = 8
CHUNK = 128
N_BRANCH = 3
ROPE_THETA = 10000.0
AXIS_DIM = DQK // 2
EPS = 1e-6
N_IN_GROUPS = 8 + 2 * N_BRANCH
LANES = 128
MXU_DIM = 256
MOD_ROWS = 16
HALO = 16
VMEM_LIMIT = 56 * 1024 * 1024

G_XA, G_GB, G_GC, G_Q, G_K, G_V, G_U, G_VG, G_GATES = range(9)
SLOT_XA, SLOT_GB, SLOT_GC, SLOT_V, SLOT_GATES = 0, 1, 2, 3, 4
N_SLOTS = SLOT_GATES + 2 * N_BRANCH

F32 = jnp.float32
BF16 = jnp.bfloat16


def _slab_slot(j):
    return jnp.where(j < G_Q, j, jnp.where(j <= G_V, SLOT_V, jnp.where(j <= G_GATES, SLOT_GATES, j - G_GATES + SLOT_GATES)))


def _params(semantics):
    return pltpu.CompilerParams(dimension_semantics=semantics, vmem_limit_bytes=VMEM_LIMIT)


def _sigmoid(t):
    return 0.5 * jnp.tanh(0.5 * t) + 0.5


def _modulated_norm(x, gain, shift, scale):
    ms = jnp.mean(x * x, axis=-1, keepdims=True)
    return (x * lax.rsqrt(ms + EPS)) * (gain * (1.0 + scale)) + shift


def _mod_kernel(c_ref, w_ref, b_ref, o_ref):
    c = c_ref[...]
    a = (c * _sigmoid(c)).astype(BF16)
    o_ref[...] = jnp.dot(a, w_ref[...].astype(BF16), preferred_element_type=F32) + b_ref[...]


def _mod_call(cvec, w_mod, b_mod):
    depth = w_mod.shape[0]
    tn = 1024
    nn = N_MOD * D_MODEL // tn
    return pl.pallas_call(
        _mod_kernel,
        grid=(depth, nn),
        in_specs=[
            pl.BlockSpec((MOD_ROWS, D_MODEL), lambda l, n: (0, 0)),
            pl.BlockSpec((None, D_MODEL, tn), lambda l, n: (l, 0, n)),
            pl.BlockSpec((None, 1, tn), lambda l, n: (l, 0, n)),
        ],
        out_specs=pl.BlockSpec((None, MOD_ROWS, tn), lambda l, n: (l, 0, n)),
        out_shape=jax.ShapeDtypeStruct((depth, MOD_ROWS, N_MOD * D_MODEL), F32),
        compiler_params=_params(("parallel", "parallel")),
        name="adaln_mod",
    )(cvec, w_mod, b_mod.reshape(depth, 1, N_MOD * D_MODEL))


def _mod_spec(l, j, row_of_tile):
    return pl.BlockSpec((None, None, None, 1, D_MODEL), lambda i, n: (l, row_of_tile(i), j, 0, 0))


class _ResidualTiles:
    def __init__(self, x_hbm, o_hbm, buf_ref, sem_in, sem_out, tm):
        self.x_hbm, self.o_hbm, self.buf, self.sem_in, self.sem_out, self.tm = x_hbm, o_hbm, buf_ref, sem_in, sem_out, tm
        self.i, self.f = pl.program_id(0), pl.program_id(1)
        self.ntiles, self.nsteps = pl.num_programs(0), pl.num_programs(1)
        self.slot = self.i % 2

    def _rows(self, tile):
        return pl.ds(pl.multiple_of(tile * self.tm, self.tm), self.tm)

    def _load(self, tile, slot):
        return pltpu.make_async_copy(self.x_hbm.at[self._rows(tile)], self.buf.at[slot], self.sem_in.at[slot])

    def _store(self, tile, slot):
        return pltpu.make_async_copy(self.buf.at[slot], self.o_hbm.at[self._rows(tile)], self.sem_out.at[slot])

    def arrive(self):
        @pl.when((self.i == 0) & (self.f == 0))
        def _():
            self._load(0, 0).start()

        @pl.when(self.f == 0)
        def _():
            self._load(self.i, self.slot).wait()

    def prefetch_next(self):
        @pl.when((self.f == 1) & (self.i + 1 < self.ntiles))
        def _():
            @pl.when(self.i >= 1)
            def _():
                self._store(self.i - 1, 1 - self.slot).wait()
            self._load(self.i + 1, 1 - self.slot).start()

    def depart(self):
        last = self.f == self.nsteps - 1

        @pl.when(last)
        def _():
            self._store(self.i, self.slot).start()

        @pl.when(last & (self.i == self.ntiles - 1))
        def _():
            @pl.when(self.ntiles >= 2)
            def _():
                self._store(self.i - 1, 1 - self.slot).wait()
            self._store(self.i, self.slot).wait()


def _ffn_kernel(x_hbm, g_ref, sh_ref, sc_ref, gt_ref, wa_ref, wg_ref, wo_ref, o_hbm, res_ref, h_ref, sem_in, sem_out, *, tm):
    tiles = _ResidualTiles(x_hbm, o_hbm, res_ref, sem_in, sem_out, tm)
    tiles.arrive()
    acc = res_ref.at[tiles.slot]

    @pl.when(pl.program_id(1) == 0)
    def _():
        h_ref[...] = _modulated_norm(acc[...], g_ref[...], sh_ref[...], sc_ref[...]).astype(BF16)

    tiles.prefetch_next()
    h = h_ref[...]
    a = jnp.dot(h, wa_ref[...], preferred_element_type=F32)
    g = jnp.dot(h, wg_ref[...], preferred_element_type=F32)
    act = (a * _sigmoid(a) * g).astype(BF16)
    acc[...] += (0.5 * gt_ref[...]) * jnp.dot(act, wo_ref[...], preferred_element_type=F32)
    tiles.depart()


def _ffn_call(x, mod, norm_g, w_in, w_out, *, l, s, tm, tf, row_of_tile):
    t = x.shape[0]
    nf = D_FF // tf
    assert nf >= 2 and t % tm == 0
    mod_base = 0 if s == 0 else 6
    norm_idx = 0 if s == 0 else 2
    return pl.pallas_call(
        functools.partial(_ffn_kernel, tm=tm),
        grid=(t // tm, nf),
        in_specs=[
            pl.BlockSpec(memory_space=pl.ANY),
            pl.BlockSpec((None, None, 1, D_MODEL), lambda i, f: (l, norm_idx, 0, 0)),
            _mod_spec(l, mod_base + 0, row_of_tile),
            _mod_spec(l, mod_base + 1, row_of_tile),
            _mod_spec(l, mod_base + 2, row_of_tile),
            pl.BlockSpec((None, None, D_MODEL, tf), lambda i, f: (l, s, 0, f)),
            pl.BlockSpec((None, None, D_MODEL, tf), lambda i, f: (l, s, 0, nf + f)),
            pl.BlockSpec((None, None, tf, D_MODEL), lambda i, f: (l, s, f, 0)),
        ],
        out_specs=pl.BlockSpec(memory_space=pl.ANY),
        out_shape=jax.ShapeDtypeStruct((t, D_MODEL), F32),
        input_output_aliases={0: 0},
        scratch_shapes=[pltpu.VMEM((2, tm, D_MODEL), F32), pltpu.VMEM((tm, D_MODEL), BF16),
                        pltpu.SemaphoreType.DMA((2,)), pltpu.SemaphoreType.DMA((2,))],
        compiler_params=_params(("arbitrary", "arbitrary")),
        name=f"ffn{s}",
    )(x, norm_g, mod, mod, mod, w_in, w_in, w_out)


def _partner(y):
    lane = lax.broadcasted_iota(jnp.int32, y.shape, 1)
    ahead = pltpu.roll(y, LANES - AXIS_DIM // 2, axis=1)
    behind = pltpu.roll(y, AXIS_DIM // 2, axis=1)
    return jnp.where(lane % AXIS_DIM < AXIS_DIM // 2, ahead, behind)


def _segment_ones():
    r = lax.broadcasted_iota(jnp.int32, (MXU_DIM, MXU_DIM), 0)
    c = lax.broadcasted_iota(jnp.int32, (MXU_DIM, MXU_DIM), 1)
    return jnp.where(r // DQK == c // DQK, 1.0, 0.0).astype(BF16)


def _qk_norm(z, cos_g, sin_g, store):
    seg = _segment_ones()
    for pair in range(BRANCH_WIDTH // MXU_DIM):
        zp = z[:, pair * MXU_DIM:(pair + 1) * MXU_DIM]
        ss = jnp.dot((zp * zp).astype(BF16), seg, preferred_element_type=F32)
        r = lax.rsqrt(ss * (1.0 / DQK) + EPS)
        for half in range(MXU_DIM // LANES):
            zc = zp[:, half * LANES:(half + 1) * LANES]
            y = zc * cos_g
            if sin_g is not None:
                y = y + _partner(zc) * sin_g
            store(pair * (MXU_DIM // LANES) + half, y * r[:, half * LANES:(half + 1) * LANES])


def _mix_in_kernel(*refs, tm, seq, is_ctx):
    it = iter(refs)
    x_ref, g_ref, sh_ref, sc_ref, w_ref, qkg_ref = (next(it) for _ in range(6))
    cos_ref = sin_ref = None
    if not is_ctx:
        cos_ref, sin_ref = next(it), next(it)
    lng_ref, lnb_ref, ws_ref, bs_ref = (next(it) for _ in range(4))
    if is_ctx:
        next(it), next(it)
    slab_ref, q_ref, k_ref, yc_ref = (next(it) for _ in range(4))
    kc_ref = vc_ref = None
    if is_ctx:
        kc_ref, vc_ref = next(it), next(it)
    h_ref, ug_ref = next(it), next(it)
    tab_ref = None if is_ctx else next(it)

    j = pl.program_id(1)
    q_scale = DQK ** -0.5

    def project():
        return jnp.dot(h_ref[...], w_ref[...], preferred_element_type=F32)

    @pl.when(j == G_XA)
    def _():
        h = _modulated_norm(x_ref[...], g_ref[...], sh_ref[...], sc_ref[...]).astype(BF16)
        h_ref[...] = h
        if not is_ctx:
            cos, sin = cos_ref[...], sin_ref[...]
            gq = jnp.broadcast_to(qkg_ref[0, :, :LANES], cos.shape) * q_scale
            gk = jnp.broadcast_to(qkg_ref[1, :, :LANES], cos.shape)
            tab_ref[0] = cos * gq
            tab_ref[1] = sin * _partner(gq)
            tab_ref[2] = cos * gk
            tab_ref[3] = sin * _partner(gk)
        slab_ref[...] = jnp.dot(h, w_ref[...], preferred_element_type=F32).astype(BF16)

    cast_only = (j == G_GB) | (j == G_GC) | (j >= G_GATES)
    if not is_ctx:
        cast_only = cast_only | (j == G_V)

    @pl.when(cast_only)
    def _():
        slab_ref[...] = project().astype(BF16)

    if is_ctx:
        @pl.when(j == G_V)
        def _():
            z = project()
            slab_ref[...] = z.astype(BF16)
            for b in range(tm // seq):
                vc_ref[b] = z[b * seq:(b + 1) * seq]

    def store_q(c, y):
        q_ref[:, c * LANES:(c + 1) * LANES] = y.astype(BF16)

    def store_k(c, y):
        k_ref[:, c * LANES:(c + 1) * LANES] = y.astype(BF16)
        if is_ctx:
            for b in range(tm // seq):
                kc_ref[b, :, c * LANES:(c + 1) * LANES] = y[b * seq:(b + 1) * seq]

    @pl.when(j == G_Q)
    def _():
        if is_ctx:
            _qk_norm(project(), qkg_ref[0, :, :LANES] * q_scale, None, store_q)
        else:
            _qk_norm(project(), tab_ref[0], tab_ref[1], store_q)

    @pl.when(j == G_K)
    def _():
        if is_ctx:
            _qk_norm(project(), qkg_ref[1, :, :LANES], None, store_k)
        else:
            _qk_norm(project(), tab_ref[2], tab_ref[3], store_k)

    @pl.when(j == G_U)
    def _():
        ug_ref[...] = jax.nn.gelu(project())

    @pl.when(j == G_VG)
    def _():
        gv = jax.nn.gelu(project())
        mu = jnp.mean(gv, axis=-1, keepdims=True)
        d = gv - mu
        var = jnp.mean(d * d, axis=-1, keepdims=True)
        vn = (d * lax.rsqrt(var + EPS) * lng_ref[...] + lnb_ref[...]).astype(BF16)
        nch = tm // CHUNK
        for g in range(GMLP_GROUPS):
            cols = slice(g * CHUNK, (g + 1) * CHUNK)
            stacked = jnp.concatenate([vn[n * CHUNK:(n + 1) * CHUNK, cols] for n in range(nch)], axis=1)
            mixed = jnp.dot(ws_ref[g], stacked, preferred_element_type=F32)
            for n in range(nch):
                rows = slice(n * CHUNK, (n + 1) * CHUNK)
                m = mixed[:, n * CHUNK:(n + 1) * CHUNK] + bs_ref[:, cols]
                yc_ref[rows, cols] = (ug_ref[rows, cols] * m).astype(BF16)


def _mix_in_call(x, mod, norm_g, w_in, qk_gain_t, rope, ln_g, ln_b, w_s, b_s_t, caches, *, l, tm, seq, row_of_tile):
    t = x.shape[0]
    is_ctx = caches is not None
    bw = BRANCH_WIDTH
    tok = lambda i, j: (i, 0)
    in_specs = [
        pl.BlockSpec((tm, D_MODEL), tok),
        pl.BlockSpec((None, None, 1, D_MODEL), lambda i, j: (l, 1, 0, 0)),
        _mod_spec(l, 3, row_of_tile),
        _mod_spec(l, 4, row_of_tile),
        pl.BlockSpec((None, D_MODEL, bw), lambda i, j: (l, 0, j)),
        pl.BlockSpec((None, 2, 1, bw), lambda i, j: (l, 0, 0, 0)),
    ]
    args = [x, norm_g, mod, mod, w_in, qk_gain_t]
    scratch = [pltpu.VMEM((tm, D_MODEL), BF16), pltpu.VMEM((tm, bw), F32)]
    if not is_ctx:
        tiles_per_seq = seq // tm
        in_specs += [pl.BlockSpec((tm, LANES), lambda i, j: (i % tiles_per_seq, 0))] * 2
        args += [rope[0], rope[1]]
        scratch += [pltpu.VMEM((4, tm, LANES), F32)]
    in_specs += [
        pl.BlockSpec((None, 1, bw), lambda i, j: (l, 0, 0)),
        pl.BlockSpec((None, 1, bw), lambda i, j: (l, 0, 0)),
        pl.BlockSpec((None, GMLP_GROUPS, CHUNK, CHUNK), lambda i, j: (l, 0, 0, 0)),
        pl.BlockSpec((None, CHUNK, bw), lambda i, j: (l, 0, 0)),
    ]
    args += [ln_g, ln_b, w_s, b_s_t]
    act = jax.ShapeDtypeStruct((t, bw), BF16)
    out_shape = [jax.ShapeDtypeStruct((t, N_SLOTS * bw), BF16), act, act, act]
    out_specs = [pl.BlockSpec((tm, bw), lambda i, j: (i, _slab_slot(j)))] + [pl.BlockSpec((tm, bw), tok)] * 3
    aliases = {}
    if is_ctx:
        nb = tm // seq
        cache_spec = pl.BlockSpec((nb, None, seq, bw), lambda i, j: (i, l, 0, 0))
        in_specs += [pl.BlockSpec(memory_space=pl.ANY)] * 2
        aliases = {len(args): len(out_shape), len(args) + 1: len(out_shape) + 1}
        args += list(caches)
        out_shape += [jax.ShapeDtypeStruct(c.shape, c.dtype) for c in caches]
        out_specs += [cache_spec, cache_spec]
    return pl.pallas_call(
        functools.partial(_mix_in_kernel, tm=tm, seq=seq, is_ctx=is_ctx),
        grid=(t // tm, N_IN_GROUPS),
        in_specs=in_specs,
        out_specs=out_specs,
        out_shape=out_shape,
        input_output_aliases=aliases,
        scratch_shapes=scratch,
        compiler_params=_params(("parallel", "arbitrary")),
        name="mix_in_ctx" if is_ctx else "mix_in_lat",
    )(*args)


def _lambda_init(l):
    return 0.8 - 0.6 * math.exp(-0.3 * l)


def _lambda_full(lam_ref, l):
    lp = lam_ref[...]
    a = jnp.sum(lp[0:1] * lp[1:2], axis=-1, keepdims=True)
    b = jnp.sum(lp[2:3] * lp[3:4], axis=-1, keepdims=True)
    return jnp.exp(a) - jnp.exp(b) + _lambda_init(l)


def _softmax_times_values(qm, k, vaug):
    s = lax.dot_general(qm, k, (((1,), (1,)), ((), ())), preferred_element_type=F32)
    e = jnp.exp(s - jnp.max(s, axis=-1, keepdims=True)).astype(BF16)
    pv = jnp.dot(e, vaug, preferred_element_type=F32)
    return pv[:, :DV] / pv[:, DV:]


def _attend_head(qh, kh, vaug, lam_full, subg, l):
    lane = lax.broadcasted_iota(jnp.int32, qh.shape, 1)
    zero = jnp.zeros_like(qh)
    o0 = _softmax_times_values(jnp.where(lane < DQK, qh, zero), kh, vaug)
    o1 = _softmax_times_values(jnp.where(lane < DQK, zero, qh), kh, vaug)
    o = o0 - lam_full * o1
    ms = jnp.mean(o * o, axis=-1, keepdims=True)
    return (o * lax.rsqrt(ms + EPS) * subg) * (1.0 - _lambda_init(l))


def _attn_ctx_kernel(q_ref, k_ref, v_ref, lam_ref, subg_ref, o_ref, *, l):
    lam_full = _lambda_full(lam_ref, l)
    ones = jnp.ones((k_ref.shape[0], DV), BF16)
    for h in range(N_HEADS):
        cols = slice(h * DV, (h + 1) * DV)
        vaug = jnp.concatenate([v_ref[:, cols], ones], axis=1)
        o_ref[:, cols] = _attend_head(q_ref[:, cols], k_ref[:, cols], vaug, lam_full, subg_ref[...], l).astype(BF16)


def _attn_ctx_call(q, k, slab, lam, subln_g, *, l, seq):
    t = q.shape[0]
    blk = pl.BlockSpec((seq, BRANCH_WIDTH), lambda b: (b, 0))
    return pl.pallas_call(
        functools.partial(_attn_ctx_kernel, l=l),
        grid=(t // seq,),
        in_specs=[blk, blk, pl.BlockSpec((seq, BRANCH_WIDTH), lambda b: (b, SLOT_V)),
                  pl.BlockSpec((None, 4, DQK), lambda b: (l, 0, 0)),
                  pl.BlockSpec((None, 1, DV), lambda b: (l, 0, 0))],
        out_specs=blk,
        out_shape=jax.ShapeDtypeStruct((t, BRANCH_WIDTH), BF16),
        compiler_params=_params(("parallel",)),
        name="attn_ctx",
    )(q, k, slab, lam, subln_g)


def _attn_lat_kernel(q_ref, k_ref, v_ref, ck_ref, cv_ref, lam_ref, subg_ref, o_ref, kall_ref, vaug_ref, *, l, past):
    @pl.when(pl.program_id(1) == 0)
    def _():
        kall_ref[:past] = ck_ref[...].astype(BF16)
        kall_ref[past:] = k_ref[...]
        for h in range(N_HEADS):
            cols = slice(h * DV, (h + 1) * DV)
            vaug_ref[h, :past, :DV] = cv_ref[:, cols].astype(BF16)
            vaug_ref[h, past:, :DV] = v_ref[:, cols]
            vaug_ref[h, :, DV:] = jnp.ones((vaug_ref.shape[1], DV), BF16)

    lam_full = _lambda_full(lam_ref, l)
    for h in range(N_HEADS):
        cols = slice(h * DV, (h + 1) * DV)
        o_ref[:, cols] = _attend_head(q_ref[:, cols], kall_ref[:, cols], vaug_ref[h], lam_full, subg_ref[...], l).astype(BF16)


def _attn_lat_call(q, k, slab, cache_k, cache_v, lam, subln_g, *, l, seq, tq):
    t = q.shape[0]
    past = cache_k.shape[2]
    nq = seq // tq
    bw = BRANCH_WIDTH
    cache = pl.BlockSpec((None, None, past, bw), lambda b, i: (b, l, 0, 0))
    return pl.pallas_call(
        functools.partial(_attn_lat_kernel, l=l, past=past),
        grid=(t // seq, nq),
        in_specs=[pl.BlockSpec((tq, bw), lambda b, i: (b * nq + i, 0)),
                  pl.BlockSpec((seq, bw), lambda b, i: (b, 0)),
                  pl.BlockSpec((seq, bw), lambda b, i: (b, SLOT_V)),
                  cache, cache,
                  pl.BlockSpec((None, 4, DQK), lambda b, i: (l, 0, 0)),
                  pl.BlockSpec((None, 1, DV), lambda b, i: (l, 0, 0))],
        out_specs=pl.BlockSpec((tq, bw), lambda b, i: (b * nq + i, 0)),
        out_shape=jax.ShapeDtypeStruct((t, bw), BF16),
        scratch_shapes=[pltpu.VMEM((past + seq, bw), BF16), pltpu.VMEM((N_HEADS, past + seq, 2 * DV), BF16)],
        compiler_params=_params(("parallel", "arbitrary")),
        name="attn_lat",
    )(q, k, slab, cache_k, cache_v, lam, subln_g)


def _mix_out_kernel(x_ref, gt_ref, xa_ref, xa_prev_ref, xa_next_ref, gc_ref, gc_prev_ref, gc_next_ref, gb_ref, wc_ref,
                    yb_ref, yc_ref, za_ref, zb_ref, zc_ref, wa_ref, wb_ref, wcc_ref, wo_ref, o_ref, ya_ref, *, tm, seq):
    @pl.when(pl.program_id(1) == 0)
    def _():
        p = xa_ref[...].astype(F32) * gc_ref[...].astype(F32)
        row = lax.broadcasted_iota(jnp.int32, p.shape, 0)
        pos = (pl.program_id(0) * tm + row) % seq
        halo_prev = (xa_prev_ref[...].astype(F32) * gc_prev_ref[...].astype(F32))[HALO - 1:HALO, :]
        halo_next = (xa_next_ref[...].astype(F32) * gc_next_ref[...].astype(F32))[0:1, :]
        prev = jnp.where(row == 0, halo_prev, pltpu.roll(p, 1, axis=0))
        prev = jnp.where(pos == 0, 0.0, prev)
        nxt = jnp.where(row == tm - 1, halo_next, pltpu.roll(p, tm - 1, axis=0))
        nxt = jnp.where(pos == seq - 1, 0.0, nxt)
        conv = wc_ref[0:1, :] * prev + wc_ref[1:2, :] * p + wc_ref[2:3, :] * nxt
        ya_ref[...] = (gb_ref[...].astype(F32) * conv).astype(BF16)
        o_ref[...] = x_ref[...]

    ya = jnp.dot(ya_ref[...], wa_ref[...], preferred_element_type=F32)
    yb = jnp.dot(yb_ref[...], wb_ref[...], preferred_element_type=F32)
    yc = jnp.dot(yc_ref[...], wcc_ref[...], preferred_element_type=F32)
    merged = (_sigmoid(za_ref[...].astype(F32)) * ya + _sigmoid(zb_ref[...].astype(F32)) * yb
              + _sigmoid(zc_ref[...].astype(F32)) * yc)
    o_ref[...] += gt_ref[...] * jnp.dot(merged.astype(BF16), wo_ref[...], preferred_element_type=F32)


def _mix_out_call(x, mod, slab, w_conv, yb, yc, w_branch, w_o, *, l, tm, tn, seq, row_of_tile):
    t = x.shape[0]
    bw = BRANCH_WIDTH
    nn = D_MODEL // tn
    last_halo = t // HALO - 1
    gate0 = SLOT_GATES * bw // tn

    def main(slot):
        return pl.BlockSpec((tm, bw), lambda i, n: (i, slot))

    def prev(slot):
        return pl.BlockSpec((HALO, bw), lambda i, n: (jnp.maximum(i * (tm // HALO) - 1, 0), slot))

    def nxt(slot):
        return pl.BlockSpec((HALO, bw), lambda i, n: (jnp.minimum((i + 1) * (tm // HALO), last_halo), slot))

    def gate(branch):
        return pl.BlockSpec((tm, tn), lambda i, n: (i, gate0 + branch * nn + n))

    def branch_w(branch):
        return pl.BlockSpec((None, None, bw, tn), lambda i, n: (l, branch, 0, n))

    act = pl.BlockSpec((tm, bw), lambda i, n: (i, 0))
    return pl.pallas_call(
        functools.partial(_mix_out_kernel, tm=tm, seq=seq),
        grid=(t // tm, nn),
        in_specs=[
            pl.BlockSpec((tm, D_MODEL), lambda i, n: (i, 0)),
            _mod_spec(l, 5, row_of_tile),
            main(SLOT_XA), prev(SLOT_XA), nxt(SLOT_XA),
            main(SLOT_GC), prev(SLOT_GC), nxt(SLOT_GC),
            main(SLOT_GB),
            pl.BlockSpec((None, 3, bw), lambda i, n: (l, 0, 0)),
            act, act,
            gate(0), gate(1), gate(2),
            branch_w(0), branch_w(1), branch_w(2),
            pl.BlockSpec((None, tn, D_MODEL), lambda i, n: (l, n, 0)),
        ],
        out_specs=pl.BlockSpec((tm, D_MODEL), lambda i, n: (i, 0)),
        out_shape=jax.ShapeDtypeStruct((t, D_MODEL), F32),
        scratch_shapes=[pltpu.VMEM((tm, bw), BF16)],
        compiler_params=_params(("parallel", "arbitrary")),
        name="mix_out",
    )(x, mod, slab, slab, slab, slab, slab, slab, slab, w_conv, yb, yc, slab, slab, slab,
      w_branch, w_branch, w_branch, w_o)


def _rope_tables(seq):
    rows = seq // GRID_W
    row = jnp.repeat(jnp.arange(rows, dtype=F32), GRID_W)
    col = jnp.tile(jnp.arange(GRID_W, dtype=F32), rows)
    inv = ROPE_THETA ** (-jnp.arange(0, AXIS_DIM, 2, dtype=F32) / AXIS_DIM)
    ang_r, ang_c = row[:, None] * inv, col[:, None] * inv
    cos = jnp.concatenate([jnp.cos(ang_r)] * 2 + [jnp.cos(ang_c)] * 2, axis=1)
    sin = jnp.concatenate([-jnp.sin(ang_r), jnp.sin(ang_r), -jnp.sin(ang_c), jnp.sin(ang_c)], axis=1)
    return jnp.tile(cos, (1, LANES // DQK)), jnp.tile(sin, (1, LANES // DQK))


def kernel(x_prompt, x_sample, c, c_ctx, cache_k, cache_v, w_mod, b_mod, norm_g, w_ffn_in, w_ffn_out,
           w_in, w_conv, qk_gain, lam, subln_g, gmlp_ln_g, gmlp_ln_b, w_s, b_s, w_branch, w_o):
    batch, seq, _ = x_prompt.shape
    dec_batch, dec_seq, _ = x_sample.shape
    depth = w_mod.shape[0]
    past = cache_k.shape[2]
    bw = BRANCH_WIDTH
    assert dec_batch + 1 <= MOD_ROWS and seq % CHUNK == 0 and dec_seq % CHUNK == 0

    def ctx_tile(want):
        nb = max(1, want // seq)
        while batch % nb:
            nb -= 1
        return nb * seq

    tm_mix = {"ctx": ctx_tile(512), "lat": min(512, dec_seq)}
    tm_ffn = {"ctx": ctx_tile(1024), "lat": min(1024, dec_seq)}
    tf = 512
    tn_out = 512
    tq = 256

    cvec = jnp.zeros((MOD_ROWS, D_MODEL), F32).at[0].set(c_ctx).at[1:1 + dec_batch].set(c)
    mod = _mod_call(cvec, w_mod, b_mod).reshape(depth, MOD_ROWS, N_MOD, 1, D_MODEL)

    w_ffn_in_b = w_ffn_in.astype(BF16)
    w_ffn_out_b = w_ffn_out.astype(BF16)
    w_in_b = w_in.astype(BF16)
    w_branch_b = w_branch.astype(BF16)
    w_o_b = w_o.astype(BF16)
    w_s_b = w_s.astype(BF16)
    norm_g4 = norm_g.reshape(depth, 3, 1, D_MODEL)
    qk_gain_t = jnp.tile(qk_gain, (1, 1, bw // DQK)).reshape(depth, 2, 1, bw)
    ln_g3 = gmlp_ln_g.reshape(depth, 1, bw)
    ln_b3 = gmlp_ln_b.reshape(depth, 1, bw)
    b_s_t = jnp.repeat(jnp.swapaxes(b_s, 1, 2), bw // GMLP_GROUPS, axis=2)
    subg3 = subln_g.reshape(depth, 1, DV)
    rope = _rope_tables(dec_seq)
    cache_k4 = cache_k.reshape(dec_batch, depth, past, bw)
    cache_v4 = cache_v.reshape(dec_batch, depth, past, bw)

    def row_map(path, tm):
        if path == "ctx":
            return lambda i: 0
        tiles_per_seq = dec_seq // tm
        return lambda i: 1 + i // tiles_per_seq

    xp = x_prompt.reshape(batch * seq, D_MODEL)
    xs = x_sample.reshape(dec_batch * dec_seq, D_MODEL)
    new_k = jnp.zeros((batch, depth, seq, bw), F32)
    new_v = jnp.zeros((batch, depth, seq, bw), F32)

    for l in range(depth):
        for path in ("ctx", "lat"):
            is_ctx = path == "ctx"
            x = xp if is_ctx else xs
            sq = seq if is_ctx else dec_seq
            tm, tmf = tm_mix[path], tm_ffn[path]
            ffn = functools.partial(_ffn_call, mod=mod, norm_g=norm_g4, w_in=w_ffn_in_b, w_out=w_ffn_out_b,
                                    l=l, tm=tmf, tf=tf, row_of_tile=row_map(path, tmf))
            x = ffn(x, s=0)
            outs = _mix_in_call(x, mod, norm_g4, w_in_b, qk_gain_t, None if is_ctx else rope, ln_g3, ln_b3, w_s_b, b_s_t,
                                (new_k, new_v) if is_ctx else None, l=l, tm=tm, seq=sq, row_of_tile=row_map(path, tm))
            slab, q, k, yc = outs[:4]
            if is_ctx:
                new_k, new_v = outs[4], outs[5]
                yb = _attn_ctx_call(q, k, slab, lam, subg3, l=l, seq=sq)
            else:
                yb = _attn_lat_call(q, k, slab, cache_k4, cache_v4, lam, subg3, l=l, seq=sq, tq=tq)
            x = _mix_out_call(x, mod, slab, w_conv, yb, yc, w_branch_b, w_o_b,
                              l=l, tm=tm, tn=tn_out, seq=sq, row_of_tile=row_map(path, tm))
            x = ffn(x, s=1)
            if is_ctx:
                xp = x
            else:
                xs = x

    return (xp.reshape(batch, seq, D_MODEL), xs.reshape(dec_batch, dec_seq, D_MODEL),
            new_k.reshape(batch, depth, seq, N_HEADS, 2, DQK), new_v.reshape(batch, depth, seq, N_HEADS, DV))
```

```python
import functools
import math

import jax
import jax.numpy as jnp
from jax import lax
from jax.experimental import pallas as pl
from jax.experimental.pallas import tpu as pltpu

D_MODEL = 2048
GRID_W = 64
N_MOD = 9
D_FF = 5632
BRANCH_WIDTH = D_MODEL // 2
N_HEADS = 8
DQK = 64
DV = 2 * DQK
GMLP_GROUPS = 8
CHUNK = 128
N_BRANCH = 3
ROPE_THETA = 10000.0
AXIS_DIM = DQK // 2
EPS = 1e-6
N_IN_GROUPS = 8 + 2 * N_BRANCH
LANES = 128
MXU_DIM = 256
MOD_ROWS = 16
HALO = 16
VMEM_LIMIT = 56 * 1024 * 1024

G_XA, G_GB, G_GC, G_Q, G_K, G_V, G_U, G_VG, G_GATES = range(9)
SLOT_XA, SLOT_GB, SLOT_GC, SLOT_V, SLOT_GATES = 0, 1, 2, 3, 4
N_SLOTS = SLOT_GATES + 2 * N_BRANCH

F32 = jnp.float32
BF16 = jnp.bfloat16


def _slab_slot(j):
    return jnp.where(j < G_Q, j, jnp.where(j <= G_V, SLOT_V, jnp.where(j <= G_GATES, SLOT_GATES, j - G_GATES + SLOT_GATES)))


def _params(semantics):
    return pltpu.CompilerParams(dimension_semantics=semantics, vmem_limit_bytes=VMEM_LIMIT)


def _sigmoid(t):
    return 0.5 * jnp.tanh(0.5 * t) + 0.5


def _modulated_norm(x, gain, shift, scale):
    ms = jnp.mean(x * x, axis=-1, keepdims=True)
    return (x * lax.rsqrt(ms + EPS)) * (gain * (1.0 + scale)) + shift


def _mod_kernel(c_ref, w_ref, b_ref, o_ref):
    c = c_ref[...]
    a = (c * _sigmoid(c)).astype(BF16)
    o_ref[...] = jnp.dot(a, w_ref[...].astype(BF16), preferred_element_type=F32) + b_ref[...]


def _mod_call(cvec, w_mod, b_mod):
    depth = w_mod.shape[0]
    tn = 1024
    nn = N_MOD * D_MODEL // tn
    return pl.pallas_call(
        _mod_kernel,
        grid=(depth, nn),
        in_specs=[
            pl.BlockSpec((MOD_ROWS, D_MODEL), lambda l, n: (0, 0)),
            pl.BlockSpec((None, D_MODEL, tn), lambda l, n: (l, 0, n)),
            pl.BlockSpec((None, 1, tn), lambda l, n: (l, 0, n)),
        ],
        out_specs=pl.BlockSpec((None, MOD_ROWS, tn), lambda l, n: (l, 0, n)),
        out_shape=jax.ShapeDtypeStruct((depth, MOD_ROWS, N_MOD * D_MODEL), F32),
        compiler_params=_params(("parallel", "parallel")),
        name="adaln_mod",
    )(cvec, w_mod, b_mod.reshape(depth, 1, N_MOD * D_MODEL))


def _mod_spec(l, j, row_of_tile):
    return pl.BlockSpec((None, None, None, 1, D_MODEL), lambda i, n: (l, row_of_tile(i), j, 0, 0))


class _ResidualTiles:
    def __init__(self, x_hbm, o_hbm, buf_ref, sem_in, sem_out, tm):
        self.x_hbm, self.o_hbm, self.buf, self.sem_in, self.sem_out, self.tm = x_hbm, o_hbm, buf_ref, sem_in, sem_out, tm
        self.i, self.f = pl.program_id(0), pl.program_id(1)
        self.ntiles, self.nsteps = pl.num_programs(0), pl.num_programs(1)
        self.slot = self.i % 2

    def _rows(self, tile):
        return pl.ds(pl.multiple_of(tile * self.tm, self.tm), self.tm)

    def _load(self, tile, slot):
        return pltpu.make_async_copy(self.x_hbm.at[self._rows(tile)], self.buf.at[slot], self.sem_in.at[slot])

    def _store(self, tile, slot):
        return pltpu.make_async_copy(self.buf.at[slot], self.o_hbm.at[self._rows(tile)], self.sem_out.at[slot])

    def arrive(self):
        @pl.when((self.i == 0) & (self.f == 0))
        def _():
            self._load(0, 0).start()

        @pl.when(self.f == 0)
        def _():
            self._load(self.i, self.slot).wait()

    def prefetch_next(self):
        @pl.when((self.f == 1) & (self.i + 1 < self.ntiles))
        def _():
            @pl.when(self.i >= 1)
            def _():
                self._store(self.i - 1, 1 - self.slot).wait()
            self._load(self.i + 1, 1 - self.slot).start()

    def depart(self):
        last = self.f == self.nsteps - 1

        @pl.when(last)
        def _():
            self._store(self.i, self.slot).start()

        @pl.when(last & (self.i == self.ntiles - 1))
        def _():
            @pl.when(self.ntiles >= 2)
            def _():
                self._store(self.i - 1, 1 - self.slot).wait()
            self._store(self.i, self.slot).wait()


def _ffn_kernel(x_hbm, g_ref, sh_ref, sc_ref, gt_ref, wa_ref, wg_ref, wo_ref, o_hbm, res_ref, h_ref, sem_in, sem_out, *, tm):
    tiles = _ResidualTiles(x_hbm, o_hbm, res_ref, sem_in, sem_out, tm)
    tiles.arrive()
    acc = res_ref.at[tiles.slot]

    @pl.when(pl.program_id(1) == 0)
    def _():
        h_ref[...] = _modulated_norm(acc[...], g_ref[...], sh_ref[...], sc_ref[...]).astype(BF16)

    tiles.prefetch_next()
    h = h_ref[...]
    a = jnp.dot(h, wa_ref[...], preferred_element_type=F32)
    g = jnp.dot(h, wg_ref[...], preferred_element_type=F32)
    act = (a * _sigmoid(a) * g).astype(BF16)
    acc[...] += (0.5 * gt_ref[...]) * jnp.dot(act, wo_ref[...], preferred_element_type=F32)
    tiles.depart()


def _ffn_call(x, mod, norm_g, w_in, w_out, *, l, s, tm, tf, row_of_tile):
    t = x.shape[0]
    nf = D_FF // tf
    assert nf >= 2 and t % tm == 0
    mod_base = 0 if s == 0 else 6
    norm_idx = 0 if s == 0 else 2
    return pl.pallas_call(
        functools.partial(_ffn_kernel, tm=tm),
        grid=(t // tm, nf),
        in_specs=[
            pl.BlockSpec(memory_space=pl.ANY),
            pl.BlockSpec((None, None, 1, D_MODEL), lambda i, f: (l, norm_idx, 0, 0)),
            _mod_spec(l, mod_base + 0, row_of_tile),
            _mod_spec(l, mod_base + 1, row_of_tile),
            _mod_spec(l, mod_base + 2, row_of_tile),
            pl.BlockSpec((None, None, D_MODEL, tf), lambda i, f: (l, s, 0, f)),
            pl.BlockSpec((None, None, D_MODEL, tf), lambda i, f: (l, s, 0, nf + f)),
            pl.BlockSpec((None, None, tf, D_MODEL), lambda i, f: (l, s, f, 0)),
        ],
        out_specs=pl.BlockSpec(memory_space=pl.ANY),
        out_shape=jax.ShapeDtypeStruct((t, D_MODEL), F32),
        input_output_aliases={0: 0},
        scratch_shapes=[pltpu.VMEM((2, tm, D_MODEL), F32), pltpu.VMEM((tm, D_MODEL), BF16),
                        pltpu.SemaphoreType.DMA((2,)), pltpu.SemaphoreType.DMA((2,))],
        compiler_params=_params(("arbitrary", "arbitrary")),
        name=f"ffn{s}",
    )(x, norm_g, mod, mod, mod, w_in, w_in, w_out)


def _partner(y):
    lane = lax.broadcasted_iota(jnp.int32, y.shape, 1)
    ahead = pltpu.roll(y, LANES - AXIS_DIM // 2, axis=1)
    behind = pltpu.roll(y, AXIS_DIM // 2, axis=1)
    return jnp.where(lane % AXIS_DIM < AXIS_DIM // 2, ahead, behind)


def _segment_ones():
    r = lax.broadcasted_iota(jnp.int32, (MXU_DIM, MXU_DIM), 0)
    c = lax.broadcasted_iota(jnp.int32, (MXU_DIM, MXU_DIM), 1)
    return jnp.where(r // DQK == c // DQK, 1.0, 0.0).astype(BF16)


def _qk_norm(z, cos_g, sin_g, store):
    seg = _segment_ones()
    for pair in range(BRANCH_WIDTH // MXU_DIM):
        zp = z[:, pair * MXU_DIM:(pair + 1) * MXU_DIM]
        ss = jnp.dot((zp * zp).astype(BF16), seg, preferred_element_type=F32)
        r = lax.rsqrt(ss * (1.0 / DQK) + EPS)
        for half in range(MXU_DIM // LANES):
            zc = zp[:, half * LANES:(half + 1) * LANES]
            y = zc * cos_g
            if sin_g is not None:
                y = y + _partner(zc) * sin_g
            store(pair * (MXU_DIM // LANES) + half, y * r[:, half * LANES:(half + 1) * LANES])


def _mix_in_kernel(*refs, tm, seq, is_ctx):
    it = iter(refs)
    x_ref, g_ref, sh_ref, sc_ref, w_ref, qkg_ref = (next(it) for _ in range(6))
    cos_ref = sin_ref = None
    if not is_ctx:
        cos_ref, sin_ref = next(it), next(it)
    lng_ref, lnb_ref, ws_ref, bs_ref = (next(it) for _ in range(4))
    if is_ctx:
        next(it), next(it)
    slab_ref, q_ref, k_ref, yc_ref = (next(it) for _ in range(4))
    kc_ref = vc_ref = None
    if is_ctx:
        kc_ref, vc_ref = next(it), next(it)
    h_ref, ug_ref, xbuf_ref, xsem = (next(it) for _ in range(4))
    tab_ref = None if is_ctx else next(it)

    i, j = pl.program_id(0), pl.program_id(1)
    q_scale = DQK ** -0.5

    def load_x(tile):
        rows = pl.ds(pl.multiple_of(tile * tm, tm), tm)
        return pltpu.make_async_copy(x_ref.at[rows], xbuf_ref, xsem.at[0])

    @pl.when((i == 0) & (j == 0))
    def _():
        load_x(0).start()

    @pl.when((j == 1) & (i + 1 < pl.num_programs(0)))
    def _():
        load_x(i + 1).start()

    def project():
        return jnp.dot(h_ref[...], w_ref[...], preferred_element_type=F32)

    @pl.when(j == G_XA)
    def _():
        load_x(i).wait()
        h = _modulated_norm(xbuf_ref[...], g_ref[...], sh_ref[...], sc_ref[...]).astype(BF16)
        h_ref[...] = h
        if not is_ctx:
            cos, sin = cos_ref[...], sin_ref[...]
            gq = jnp.broadcast_to(qkg_ref[0, :, :LANES], cos.shape) * q_scale
            gk = jnp.broadcast_to(qkg_ref[1, :, :LANES], cos.shape)
            tab_ref[0] = cos * gq
            tab_ref[1] = sin * _partner(gq)
            tab_ref[2] = cos * gk
            tab_ref[3] = sin * _partner(gk)
        slab_ref[...] = jnp.dot(h, w_ref[...], preferred_element_type=F32).astype(BF16)

    cast_only = (j == G_GB) | (j == G_GC) | (j >= G_GATES)
    if not is_ctx:
        cast_only = cast_only | (j == G_V)

    @pl.when(cast_only)
    def _():
        slab_ref[...] = project().astype(BF16)

    if is_ctx:
        @pl.when(j == G_V)
        def _():
            z = project()
            slab_ref[...] = z.astype(BF16)
            for b in range(tm // seq):
                vc_ref[b] = z[b * seq:(b + 1) * seq]

    def store_q(c, y):
        q_ref[:, c * LANES:(c + 1) * LANES] = y.astype(BF16)

    def store_k(c, y):
        k_ref[:, c * LANES:(c + 1) * LANES] = y.astype(BF16)
        if is_ctx:
            for b in range(tm // seq):
                kc_ref[b, :, c * LANES:(c + 1) * LANES] = y[b * seq:(b + 1) * seq]

    @pl.when(j == G_Q)
    def _():
        if is_ctx:
            _qk_norm(project(), qkg_ref[0, :, :LANES] * q_scale, None, store_q)
        else:
            _qk_norm(project(), tab_ref[0], tab_ref[1], store_q)

    @pl.when(j == G_K)
    def _():
        if is_ctx:
            _qk_norm(project(), qkg_ref[1, :, :LANES], None, store_k)
        else:
            _qk_norm(project(), tab_ref[2], tab_ref[3], store_k)

    @pl.when(j == G_U)
    def _():
        ug_ref[...] = jax.nn.gelu(project())

    @pl.when(j == G_VG)
    def _():
        gv = jax.nn.gelu(project())
        mu = jnp.mean(gv, axis=-1, keepdims=True)
        d = gv - mu
        var = jnp.mean(d * d, axis=-1, keepdims=True)
        vn = (d * lax.rsqrt(var + EPS) * lng_ref[...] + lnb_ref[...]).astype(BF16)
        nch = tm // CHUNK
        for g in range(GMLP_GROUPS):
            cols = slice(g * CHUNK, (g + 1) * CHUNK)
            stacked = jnp.concatenate([vn[n * CHUNK:(n + 1) * CHUNK, cols] for n in range(nch)], axis=1)
            mixed = jnp.dot(ws_ref[g], stacked, preferred_element_type=F32)
            for n in range(nch):
                rows = slice(n * CHUNK, (n + 1) * CHUNK)
                m = mixed[:, n * CHUNK:(n + 1) * CHUNK] + bs_ref[:, cols]
                yc_ref[rows, cols] = (ug_ref[rows, cols] * m).astype(BF16)


def _mix_in_call(x, mod, norm_g, w_in, qk_gain_t, rope, ln_g, ln_b, w_s, b_s_t, caches, *, l, tm, seq, row_of_tile):
    t = x.shape[0]
    is_ctx = caches is not None
    bw = BRANCH_WIDTH
    tok = lambda i, j: (i, 0)
    in_specs = [
        pl.BlockSpec(memory_space=pl.ANY),
        pl.BlockSpec((None, None, 1, D_MODEL), lambda i, j: (l, 1, 0, 0)),
        _mod_spec(l, 3, row_of_tile),
        _mod_spec(l, 4, row_of_tile),
        pl.BlockSpec((None, D_MODEL, bw), lambda i, j: (l, 0, j)),
        pl.BlockSpec((None, 2, 1, bw), lambda i, j: (l, 0, 0, 0)),
    ]
    args = [x, norm_g, mod, mod, w_in, qk_gain_t]
    scratch = [pltpu.VMEM((tm, D_MODEL), BF16), pltpu.VMEM((tm, bw), F32),
               pltpu.VMEM((tm, D_MODEL), F32), pltpu.SemaphoreType.DMA((1,))]
    if not is_ctx:
        tiles_per_seq = seq // tm
        in_specs += [pl.BlockSpec((tm, LANES), lambda i, j: (i % tiles_per_seq, 0))] * 2
        args += [rope[0], rope[1]]
        scratch += [pltpu.VMEM((4, tm, LANES), F32)]
    in_specs += [
        pl.BlockSpec((None, 1, bw), lambda i, j: (l, 0, 0)),
        pl.BlockSpec((None, 1, bw), lambda i, j: (l, 0, 0)),
        pl.BlockSpec((None, GMLP_GROUPS, CHUNK, CHUNK), lambda i, j: (l, 0, 0, 0)),
        pl.BlockSpec((None, CHUNK, bw), lambda i, j: (l, 0, 0)),
    ]
    args += [ln_g, ln_b, w_s, b_s_t]
    act = jax.ShapeDtypeStruct((t, bw), BF16)
    out_shape = [jax.ShapeDtypeStruct((t, N_SLOTS * bw), BF16), act, act, act]
    out_specs = [pl.BlockSpec((tm, bw), lambda i, j: (i, _slab_slot(j)))] + [pl.BlockSpec((tm, bw), tok)] * 3
    aliases = {}
    if is_ctx:
        nb = tm // seq
        cache_spec = pl.BlockSpec((nb, None, seq, bw), lambda i, j: (i, l, 0, 0))
        in_specs += [pl.BlockSpec(memory_space=pl.ANY)] * 2
        aliases = {len(args): len(out_shape), len(args) + 1: len(out_shape) + 1}
        args += list(caches)
        out_shape += [jax.ShapeDtypeStruct(c.shape, c.dtype) for c in caches]
        out_specs += [cache_spec, cache_spec]
    return pl.pallas_call(
        functools.partial(_mix_in_kernel, tm=tm, seq=seq, is_ctx=is_ctx),
        grid=(t // tm, N_IN_GROUPS),
        in_specs=in_specs,
        out_specs=out_specs,
        out_shape=out_shape,
        input_output_aliases=aliases,
        scratch_shapes=scratch,
        compiler_params=_params(("arbitrary", "arbitrary")),
        name="mix_in_ctx" if is_ctx else "mix_in_lat",
    )(*args)


def _lambda_init(l):
    return 0.8 - 0.6 * math.exp(-0.3 * l)


def _lambda_full(lam_ref, l):
    lp = lam_ref[...]
    a = jnp.sum(lp[0:1] * lp[1:2], axis=-1, keepdims=True)
    b = jnp.sum(lp[2:3] * lp[3:4], axis=-1, keepdims=True)
    return jnp.exp(a) - jnp.exp(b) + _lambda_init(l)


def _softmax_times_values(qm, k, vaug):
    s = lax.dot_general(qm, k, (((1,), (1,)), ((), ())), preferred_element_type=F32)
    e = jnp.exp(s - jnp.max(s, axis=-1, keepdims=True)).astype(BF16)
    pv = jnp.dot(e, vaug, preferred_element_type=F32)
    return pv[:, :DV] / pv[:, DV:]


def _attend_head(qh, kh, vaug, lam_full, subg, l):
    lane = lax.broadcasted_iota(jnp.int32, qh.shape, 1)
    zero = jnp.zeros_like(qh)
    o0 = _softmax_times_values(jnp.where(lane < DQK, qh, zero), kh, vaug)
    o1 = _softmax_times_values(jnp.where(lane < DQK, zero, qh), kh, vaug)
    o = o0 - lam_full * o1
    ms = jnp.mean(o * o, axis=-1, keepdims=True)
    return (o * lax.rsqrt(ms + EPS) * subg) * (1.0 - _lambda_init(l))


def _attn_ctx_kernel(q_ref, k_ref, v_ref, lam_ref, subg_ref, o_ref, *, l):
    lam_full = _lambda_full(lam_ref, l)
    ones = jnp.ones((k_ref.shape[0], DV), BF16)
    for h in range(N_HEADS):
        cols = slice(h * DV, (h + 1) * DV)
        vaug = jnp.concatenate([v_ref[:, cols], ones], axis=1)
        o_ref[:, cols] = _attend_head(q_ref[:, cols], k_ref[:, cols], vaug, lam_full, subg_ref[...], l).astype(BF16)


def _attn_ctx_call(q, k, slab, lam, subln_g, *, l, seq):
    t = q.shape[0]
    blk = pl.BlockSpec((seq, BRANCH_WIDTH), lambda b: (b, 0))
    return pl.pallas_call(
        functools.partial(_attn_ctx_kernel, l=l),
        grid=(t // seq,),
        in_specs=[blk, blk, pl.BlockSpec((seq, BRANCH_WIDTH), lambda b: (b, SLOT_V)),
                  pl.BlockSpec((None, 4, DQK), lambda b: (l, 0, 0)),
                  pl.BlockSpec((None, 1, DV), lambda b: (l, 0, 0))],
        out_specs=blk,
        out_shape=jax.ShapeDtypeStruct((t, BRANCH_WIDTH), BF16),
        compiler_params=_params(("parallel",)),
        name="attn_ctx",
    )(q, k, slab, lam, subln_g)


def _attn_lat_kernel(q_ref, k_ref, v_ref, ck_ref, cv_ref, lam_ref, subg_ref, o_ref, kall_ref, vaug_ref, *, l, past):
    @pl.when(pl.program_id(1) == 0)
    def _():
        kall_ref[:past] = ck_ref[...].astype(BF16)
        kall_ref[past:] = k_ref[...]
        for h in range(N_HEADS):
            cols = slice(h * DV, (h + 1) * DV)
            vaug_ref[h, :past, :DV] = cv_ref[:, cols].astype(BF16)
            vaug_ref[h, past:, :DV] = v_ref[:, cols]
            vaug_ref[h, :, DV:] = jnp.ones((vaug_ref.shape[1], DV), BF16)

    lam_full = _lambda_full(lam_ref, l)
    for h in range(N_HEADS):
        cols = slice(h * DV, (h + 1) * DV)
        o_ref[:, cols] = _attend_head(q_ref[:, cols], kall_ref[:, cols], vaug_ref[h], lam_full, subg_ref[...], l).astype(BF16)


def _attn_lat_call(q, k, slab, cache_k, cache_v, lam, subln_g, *, l, seq, tq):
    t = q.shape[0]
    past = cache_k.shape[2]
    nq = seq // tq
    bw = BRANCH_WIDTH
    cache = pl.BlockSpec((None, None, past, bw), lambda b, i: (b, l, 0, 0))
    return pl.pallas_call(
        functools.partial(_attn_lat_kernel, l=l, past=past),
        grid=(t // seq, nq),
        in_specs=[pl.BlockSpec((tq, bw), lambda b, i: (b * nq + i, 0)),
                  pl.BlockSpec((seq, bw), lambda b, i: (b, 0)),
                  pl.BlockSpec((seq, bw), lambda b, i: (b, SLOT_V)),
                  cache, cache,
                  pl.BlockSpec((None, 4, DQK), lambda b, i: (l, 0, 0)),
                  pl.BlockSpec((None, 1, DV), lambda b, i: (l, 0, 0))],
        out_specs=pl.BlockSpec((tq, bw), lambda b, i: (b * nq + i, 0)),
        out_shape=jax.ShapeDtypeStruct((t, bw), BF16),
        scratch_shapes=[pltpu.VMEM((past + seq, bw), BF16), pltpu.VMEM((N_HEADS, past + seq, 2 * DV), BF16)],
        compiler_params=_params(("parallel", "arbitrary")),
        name="attn_lat",
    )(q, k, slab, cache_k, cache_v, lam, subln_g)


def _conv_kernel(xa_ref, gc_ref, gb_ref, wc_ref, o_ref, *, seq):
    p = xa_ref[...].astype(F32) * gc_ref[...].astype(F32)
    rows = p.shape[0]
    pos = lax.broadcasted_iota(jnp.int32, p.shape, 0) % seq
    prev = jnp.where(pos == 0, 0.0, pltpu.roll(p, 1, axis=0))
    nxt = jnp.where(pos == seq - 1, 0.0, pltpu.roll(p, rows - 1, axis=0))
    conv = wc_ref[0:1, :] * prev + wc_ref[1:2, :] * p + wc_ref[2:3, :] * nxt
    o_ref[...] = (gb_ref[...].astype(F32) * conv).astype(BF16)


def _conv_call(slab, w_conv, *, l, rows, tc, seq):
    t = slab.shape[0]
    bw = BRANCH_WIDTH
    nc = bw // tc
    assert rows % seq == 0 and t % rows == 0

    def col(slot):
        return pl.BlockSpec((rows, tc), lambda i, c: (i, slot * nc + c))

    return pl.pallas_call(
        functools.partial(_conv_kernel, seq=seq),
        grid=(t // rows, nc),
        in_specs=[col(SLOT_XA), col(SLOT_GC), col(SLOT_GB), pl.BlockSpec((None, 3, tc), lambda i, c: (l, 0, c))],
        out_specs=pl.BlockSpec((rows, tc), lambda i, c: (i, c)),
        out_shape=jax.ShapeDtypeStruct((t, bw), BF16),
        compiler_params=_params(("parallel", "parallel")),
        name="short_conv",
    )(slab, slab, slab, w_conv)


def _mix_out_kernel(x_hbm, gt_ref, ya_ref, yb_ref, yc_ref, za_ref, zb_ref, zc_ref, wa_ref, wb_ref, wcc_ref, wo_ref,
                    o_hbm, res_ref, sem_in, sem_out, *, tm):
    tiles = _ResidualTiles(x_hbm, o_hbm, res_ref, sem_in, sem_out, tm)
    tiles.arrive()
    tiles.prefetch_next()
    acc = res_ref.at[tiles.slot]
    ya = jnp.dot(ya_ref[...], wa_ref[...], preferred_element_type=F32)
    yb = jnp.dot(yb_ref[...], wb_ref[...], preferred_element_type=F32)
    yc = jnp.dot(yc_ref[...], wcc_ref[...], preferred_element_type=F32)
    merged = (_sigmoid(za_ref[...].astype(F32)) * ya + _sigmoid(zb_ref[...].astype(F32)) * yb
              + _sigmoid(zc_ref[...].astype(F32)) * yc)
    acc[...] += gt_ref[...] * jnp.dot(merged.astype(BF16), wo_ref[...], preferred_element_type=F32)
    tiles.depart()


def _mix_out_call(x, mod, slab, ya, yb, yc, w_branch, w_o, *, l, tm, tn, row_of_tile):
    t = x.shape[0]
    bw = BRANCH_WIDTH
    nn = D_MODEL // tn
    assert nn >= 2 and t % tm == 0
    gate0 = SLOT_GATES * bw // tn

    def gate(branch):
        return pl.BlockSpec((tm, tn), lambda i, n: (i, gate0 + branch * nn + n))

    def branch_w(branch):
        return pl.BlockSpec((None, None, bw, tn), lambda i, n: (l, branch, 0, n))

    act = pl.BlockSpec((tm, bw), lambda i, n: (i, 0))
    return pl.pallas_call(
        functools.partial(_mix_out_kernel, tm=tm),
        grid=(t // tm, nn),
        in_specs=[
            pl.BlockSpec(memory_space=pl.ANY),
            _mod_spec(l, 5, row_of_tile),
            act, act, act,
            gate(0), gate(1), gate(2),
            branch_w(0), branch_w(1), branch_w(2),
            pl.BlockSpec((None, tn, D_MODEL), lambda i, n: (l, n, 0)),
        ],
        out_specs=pl.BlockSpec(memory_space=pl.ANY),
        out_shape=jax.ShapeDtypeStruct((t, D_MODEL), F32),
        input_output_aliases={0: 0},
        scratch_shapes=[pltpu.VMEM((2, tm, D_MODEL), F32), pltpu.SemaphoreType.DMA((2,)), pltpu.SemaphoreType.DMA((2,))],
        compiler_params=_params(("arbitrary", "arbitrary")),
        name="mix_out",
    )(x, mod, ya, yb, yc, slab, slab, slab, w_branch, w_branch, w_branch, w_o)


def _rope_tables(seq):
    rows = seq // GRID_W
    row = jnp.repeat(jnp.arange(rows, dtype=F32), GRID_W)
    col = jnp.tile(jnp.arange(GRID_W, dtype=F32), rows)
    inv = ROPE_THETA ** (-jnp.arange(0, AXIS_DIM, 2, dtype=F32) / AXIS_DIM)
    ang_r, ang_c = row[:, None] * inv, col[:, None] * inv
    cos = jnp.concatenate([jnp.cos(ang_r)] * 2 + [jnp.cos(ang_c)] * 2, axis=1)
    sin = jnp.concatenate([-jnp.sin(ang_r), jnp.sin(ang_r), -jnp.sin(ang_c), jnp.sin(ang_c)], axis=1)
    return jnp.tile(cos, (1, LANES // DQK)), jnp.tile(sin, (1, LANES // DQK))


def kernel(x_prompt, x_sample, c, c_ctx, cache_k, cache_v, w_mod, b_mod, norm_g, w_ffn_in, w_ffn_out,
           w_in, w_conv, qk_gain, lam, subln_g, gmlp_ln_g, gmlp_ln_b, w_s, b_s, w_branch, w_o):
    batch, seq, _ = x_prompt.shape
    dec_batch, dec_seq, _ = x_sample.shape
    depth = w_mod.shape[0]
    past = cache_k.shape[2]
    bw = BRANCH_WIDTH
    assert dec_batch + 1 <= MOD_ROWS and seq % CHUNK == 0 and dec_seq % CHUNK == 0

    def ctx_tile(want):
        nb = max(1, want // seq)
        while batch % nb:
            nb -= 1
        return nb * seq

    tm_in = {"ctx": ctx_tile(512), "lat": min(1024, dec_seq)}
    tm_res = {"ctx": ctx_tile(1024), "lat": min(1024, dec_seq)}
    conv_rows = {"ctx": ctx_tile(2048), "lat": dec_seq}
    tf = 512
    tn_out = 512
    tc_conv = 256
    tq = 256

    cvec = jnp.zeros((MOD_ROWS, D_MODEL), F32).at[0].set(c_ctx).at[1:1 + dec_batch].set(c)
    mod = _mod_call(cvec, w_mod, b_mod).reshape(depth, MOD_ROWS, N_MOD, 1, D_MODEL)

    w_ffn_in_b = w_ffn_in.astype(BF16)
    w_ffn_out_b = w_ffn_out.astype(BF16)
    w_in_b = w_in.astype(BF16)
    w_branch_b = w_branch.astype(BF16)
    w_o_b = w_o.astype(BF16)
    w_s_b = w_s.astype(BF16)
    norm_g4 = norm_g.reshape(depth, 3, 1, D_MODEL)
    qk_gain_t = jnp.tile(qk_gain, (1, 1, bw // DQK)).reshape(depth, 2, 1, bw)
    ln_g3 = gmlp_ln_g.reshape(depth, 1, bw)
    ln_b3 = gmlp_ln_b.reshape(depth, 1, bw)
    b_s_t = jnp.repeat(jnp.swapaxes(b_s, 1, 2), bw // GMLP_GROUPS, axis=2)
    subg3 = subln_g.reshape(depth, 1, DV)
    rope = _rope_tables(dec_seq)
    cache_k4 = cache_k.reshape(dec_batch, depth, past, bw)
    cache_v4 = cache_v.reshape(dec_batch, depth, past, bw)

    def row_map(path, tm):
        if path == "ctx":
            return lambda i: 0
        tiles_per_seq = dec_seq // tm
        return lambda i: 1 + i // tiles_per_seq

    xp = x_prompt.reshape(batch * seq, D_MODEL)
    xs = x_sample.reshape(dec_batch * dec_seq, D_MODEL)
    new_k = jnp.zeros((batch, depth, seq, bw), F32)
    new_v = jnp.zeros((batch, depth, seq, bw), F32)

    for l in range(depth):
        for path in ("ctx", "lat"):
            is_ctx = path == "ctx"
            x = xp if is_ctx else xs
            sq = seq if is_ctx else dec_seq
            tmi, tmr = tm_in[path], tm_res[path]
            ffn = functools.partial(_ffn_call, mod=mod, norm_g=norm_g4, w_in=w_ffn_in_b, w_out=w_ffn_out_b,
                                    l=l, tm=tmr, tf=tf, row_of_tile=row_map(path, tmr))
            x = ffn(x, s=0)
            outs = _mix_in_call(x, mod, norm_g4, w_in_b, qk_gain_t, None if is_ctx else rope, ln_g3, ln_b3, w_s_b, b_s_t,
                                (new_k, new_v) if is_ctx else None, l=l, tm=tmi, seq=sq, row_of_tile=row_map(path, tmi))
            slab, q, k, yc = outs[:4]
            ya = _conv_call(slab, w_conv, l=l, rows=conv_rows[path], tc=tc_conv, seq=sq)
            if is_ctx:
                new_k, new_v = outs[4], outs[5]
                yb = _attn_ctx_call(q, k, slab, lam, subg3, l=l, seq=sq)
            else:
                yb = _attn_lat_call(q, k, slab, cache_k4, cache_v4, lam, subg3, l=l, seq=sq, tq=tq)
            x = _mix_out_call(x, mod, slab, ya, yb, yc, w_branch_b, w_o_b,
                              l=l, tm=tmr, tn=tn_out, row_of_tile=row_map(path, tmr))
            x = ffn(x, s=1)
            if is_ctx:
                xp = x
            else:
                xs = x

    return (xp.reshape(batch, seq, D_MODEL), xs.reshape(dec_batch, dec_seq, D_MODEL),
            new_k.reshape(batch, depth, seq, N_HEADS, 2, DQK), new_v.reshape(batch, depth, seq, N_HEADS, DV))
```

```python
import functools
import math

import jax
import jax.numpy as jnp
from jax import lax
from jax.experimental import pallas as pl
from jax.experimental.pallas import tpu as pltpu

D_MODEL = 2048
GRID_W = 64
N_MOD = 9
D_FF = 5632
BRANCH_WIDTH = D_MODEL // 2
N_HEADS = 8
DQK = 64
DV = 2 * DQK
GMLP_GROUPS = 8
CHUNK = 128
N_BRANCH = 3
ROPE_THETA = 10000.0
AXIS_DIM = DQK // 2
EPS = 1e-6
N_IN_GROUPS = 8 + 2 * N_BRANCH
LANES = 128
MXU_DIM = 256
MOD_ROWS = 16
HALO = 16
VMEM_LIMIT = 56 * 1024 * 1024

G_XA, G_GB, G_GC, G_Q, G_K, G_V, G_U, G_VG, G_GATES = range(9)
SLOT_XA, SLOT_GB, SLOT_GC, SLOT_V, SLOT_GATES = 0, 1, 2, 3, 4
N_SLOTS = SLOT_GATES + 2 * N_BRANCH

F32 = jnp.float32
BF16 = jnp.bfloat16


def _slab_slot(j):
    return jnp.where(j < G_Q, j, jnp.where(j <= G_V, SLOT_V, jnp.where(j <= G_GATES, SLOT_GATES, j - G_GATES + SLOT_GATES)))


def _params(semantics):
    return pltpu.CompilerParams(dimension_semantics=semantics, vmem_limit_bytes=VMEM_LIMIT)


def _sigmoid(t):
    return 0.5 * jnp.tanh(0.5 * t) + 0.5


def _modulated_norm(x, gain, shift, scale):
    ms = jnp.mean(x * x, axis=-1, keepdims=True)
    return (x * lax.rsqrt(ms + EPS)) * (gain * (1.0 + scale)) + shift


def _mod_kernel(c_ref, w_ref, b_ref, o_ref):
    c = c_ref[...]
    a = (c * _sigmoid(c)).astype(BF16)
    o_ref[...] = jnp.dot(a, w_ref[...].astype(BF16), preferred_element_type=F32) + b_ref[...]


def _mod_call(cvec, w_mod, b_mod):
    depth = w_mod.shape[0]
    tn = 1024
    nn = N_MOD * D_MODEL // tn
    return pl.pallas_call(
        _mod_kernel,
        grid=(depth, nn),
        in_specs=[
            pl.BlockSpec((MOD_ROWS, D_MODEL), lambda l, n: (0, 0)),
            pl.BlockSpec((None, D_MODEL, tn), lambda l, n: (l, 0, n)),
            pl.BlockSpec((None, 1, tn), lambda l, n: (l, 0, n)),
        ],
        out_specs=pl.BlockSpec((None, MOD_ROWS, tn), lambda l, n: (l, 0, n)),
        out_shape=jax.ShapeDtypeStruct((depth, MOD_ROWS, N_MOD * D_MODEL), F32),
        compiler_params=_params(("parallel", "parallel")),
        name="adaln_mod",
    )(cvec, w_mod, b_mod.reshape(depth, 1, N_MOD * D_MODEL))


def _mod_spec(l, j, row_of_tile):
    return pl.BlockSpec((None, None, None, 1, D_MODEL), lambda i, n: (l, row_of_tile(i), j, 0, 0))


class _ResidualTiles:
    def __init__(self, x_hbm, o_hbm, buf_ref, sem_in, sem_out, tm):
        self.x_hbm, self.o_hbm, self.buf, self.sem_in, self.sem_out, self.tm = x_hbm, o_hbm, buf_ref, sem_in, sem_out, tm
        self.i, self.f = pl.program_id(0), pl.program_id(1)
        self.ntiles, self.nsteps = pl.num_programs(0), pl.num_programs(1)
        self.slot = self.i % 2

    def _rows(self, tile):
        return pl.ds(pl.multiple_of(tile * self.tm, self.tm), self.tm)

    def _load(self, tile, slot):
        return pltpu.make_async_copy(self.x_hbm.at[self._rows(tile)], self.buf.at[slot], self.sem_in.at[slot])

    def _store(self, tile, slot):
        return pltpu.make_async_copy(self.buf.at[slot], self.o_hbm.at[self._rows(tile)], self.sem_out.at[slot])

    def arrive(self):
        @pl.when((self.i == 0) & (self.f == 0))
        def _():
            self._load(0, 0).start()

        @pl.when(self.f == 0)
        def _():
            self._load(self.i, self.slot).wait()

    def prefetch_next(self):
        @pl.when((self.f == 1) & (self.i + 1 < self.ntiles))
        def _():
            @pl.when(self.i >= 1)
            def _():
                self._store(self.i - 1, 1 - self.slot).wait()
            self._load(self.i + 1, 1 - self.slot).start()

    def depart(self):
        last = self.f == self.nsteps - 1

        @pl.when(last)
        def _():
            self._store(self.i, self.slot).start()

        @pl.when(last & (self.i == self.ntiles - 1))
        def _():
            @pl.when(self.ntiles >= 2)
            def _():
                self._store(self.i - 1, 1 - self.slot).wait()
            self._store(self.i, self.slot).wait()


def _ffn_kernel(x_hbm, g_ref, sh_ref, sc_ref, gt_ref, wa_ref, wg_ref, wo_ref, o_hbm, res_ref, h_ref, sem_in, sem_out, *, tm):
    tiles = _ResidualTiles(x_hbm, o_hbm, res_ref, sem_in, sem_out, tm)
    tiles.arrive()
    acc = res_ref.at[tiles.slot]

    def accumulate(h):
        a = jnp.dot(h, wa_ref[...], preferred_element_type=F32)
        g = jnp.dot(h, wg_ref[...], preferred_element_type=F32)
        act = (a * _sigmoid(a) * g).astype(BF16)
        acc[...] += (0.5 * gt_ref[...]) * jnp.dot(act, wo_ref[...], preferred_element_type=F32)

    @pl.when(pl.program_id(1) == 0)
    def _():
        h = _modulated_norm(acc[...], g_ref[...], sh_ref[...], sc_ref[...]).astype(BF16)
        h_ref[...] = h
        accumulate(h)

    tiles.prefetch_next()

    @pl.when(pl.program_id(1) > 0)
    def _():
        accumulate(h_ref[...])

    tiles.depart()


def _ffn_call(x, mod, norm_g, w_in, w_out, *, l, s, tm, tf, row_of_tile, in_place=True):
    t = x.shape[0]
    nf = D_FF // tf
    assert nf >= 2 and t % tm == 0
    mod_base = 0 if s == 0 else 6
    norm_idx = 0 if s == 0 else 2
    return pl.pallas_call(
        functools.partial(_ffn_kernel, tm=tm),
        grid=(t // tm, nf),
        in_specs=[
            pl.BlockSpec(memory_space=pl.ANY),
            pl.BlockSpec((None, None, 1, D_MODEL), lambda i, f: (l, norm_idx, 0, 0)),
            _mod_spec(l, mod_base + 0, row_of_tile),
            _mod_spec(l, mod_base + 1, row_of_tile),
            _mod_spec(l, mod_base + 2, row_of_tile),
            pl.BlockSpec((None, None, D_MODEL, tf), lambda i, f: (l, s, 0, f)),
            pl.BlockSpec((None, None, D_MODEL, tf), lambda i, f: (l, s, 0, nf + f)),
            pl.BlockSpec((None, None, tf, D_MODEL), lambda i, f: (l, s, f, 0)),
        ],
        out_specs=pl.BlockSpec(memory_space=pl.ANY),
        out_shape=jax.ShapeDtypeStruct((t, D_MODEL), F32),
        input_output_aliases={0: 0} if in_place else {},
        scratch_shapes=[pltpu.VMEM((2, tm, D_MODEL), F32), pltpu.VMEM((tm, D_MODEL), BF16),
                        pltpu.SemaphoreType.DMA((2,)), pltpu.SemaphoreType.DMA((2,))],
        compiler_params=_params(("arbitrary", "arbitrary")),
        name=f"ffn{s}",
    )(x, norm_g, mod, mod, mod, w_in, w_in, w_out)


def _partner(y):
    lane = lax.broadcasted_iota(jnp.int32, y.shape, 1)
    ahead = pltpu.roll(y, LANES - AXIS_DIM // 2, axis=1)
    behind = pltpu.roll(y, AXIS_DIM // 2, axis=1)
    return jnp.where(lane % AXIS_DIM < AXIS_DIM // 2, ahead, behind)


def _segment_ones():
    r = lax.broadcasted_iota(jnp.int32, (MXU_DIM, MXU_DIM), 0)
    c = lax.broadcasted_iota(jnp.int32, (MXU_DIM, MXU_DIM), 1)
    return jnp.where(r // DQK == c // DQK, 1.0, 0.0).astype(BF16)


def _qk_norm(z, cos_g, sin_g, store):
    seg = _segment_ones()
    for pair in range(BRANCH_WIDTH // MXU_DIM):
        zp = z[:, pair * MXU_DIM:(pair + 1) * MXU_DIM]
        ss = jnp.dot((zp * zp).astype(BF16), seg, preferred_element_type=F32)
        r = lax.rsqrt(ss * (1.0 / DQK) + EPS)
        for half in range(MXU_DIM // LANES):
            zc = zp[:, half * LANES:(half + 1) * LANES]
            y = zc * cos_g
            if sin_g is not None:
                y = y + _partner(zc) * sin_g
            store(pair * (MXU_DIM // LANES) + half, y * r[:, half * LANES:(half + 1) * LANES])


def _mix_in_kernel(*refs, tm, seq, is_ctx):
    it = iter(refs)
    x_ref, g_ref, sh_ref, sc_ref, w_ref, qkg_ref = (next(it) for _ in range(6))
    cos_ref = sin_ref = None
    if not is_ctx:
        cos_ref, sin_ref = next(it), next(it)
    lng_ref, lnb_ref, ws_ref, bs_ref = (next(it) for _ in range(4))
    if is_ctx:
        next(it), next(it)
    slab_ref, q_ref, k_ref, yc_ref = (next(it) for _ in range(4))
    kc_ref = vc_ref = None
    if is_ctx:
        kc_ref, vc_ref = next(it), next(it)
    h_ref, ug_ref, xbuf_ref, xsem = (next(it) for _ in range(4))
    tab_ref = None if is_ctx else next(it)

    i, j = pl.program_id(0), pl.program_id(1)
    q_scale = DQK ** -0.5

    def load_x(tile):
        rows = pl.ds(pl.multiple_of(tile * tm, tm), tm)
        return pltpu.make_async_copy(x_ref.at[rows], xbuf_ref, xsem.at[0])

    @pl.when((i == 0) & (j == 0))
    def _():
        load_x(0).start()

    @pl.when((j == 1) & (i + 1 < pl.num_programs(0)))
    def _():
        load_x(i + 1).start()

    def project():
        return jnp.dot(h_ref[...], w_ref[...], preferred_element_type=F32)

    @pl.when(j == G_XA)
    def _():
        load_x(i).wait()
        h = _modulated_norm(xbuf_ref[...], g_ref[...], sh_ref[...], sc_ref[...]).astype(BF16)
        h_ref[...] = h
        if not is_ctx:
            cos, sin = cos_ref[...], sin_ref[...]
            gq = jnp.broadcast_to(qkg_ref[0, :, :LANES], cos.shape) * q_scale
            gk = jnp.broadcast_to(qkg_ref[1, :, :LANES], cos.shape)
            tab_ref[0] = cos * gq
            tab_ref[1] = sin * _partner(gq)
            tab_ref[2] = cos * gk
            tab_ref[3] = sin * _partner(gk)
        slab_ref[...] = jnp.dot(h, w_ref[...], preferred_element_type=F32).astype(BF16)

    cast_only = (j == G_GB) | (j == G_GC) | (j >= G_GATES)
    if not is_ctx:
        cast_only = cast_only | (j == G_V)

    @pl.when(cast_only)
    def _():
        slab_ref[...] = project().astype(BF16)

    if is_ctx:
        @pl.when(j == G_V)
        def _():
            z = project()
            slab_ref[...] = z.astype(BF16)
            for b in range(tm // seq):
                vc_ref[b] = z[b * seq:(b + 1) * seq]

    def store_q(c, y):
        q_ref[:, c * LANES:(c + 1) * LANES] = y.astype(BF16)

    def store_k(c, y):
        k_ref[:, c * LANES:(c + 1) * LANES] = y.astype(BF16)
        if is_ctx:
            for b in range(tm // seq):
                kc_ref[b, :, c * LANES:(c + 1) * LANES] = y[b * seq:(b + 1) * seq]

    @pl.when(j == G_Q)
    def _():
        if is_ctx:
            _qk_norm(project(), qkg_ref[0, :, :LANES] * q_scale, None, store_q)
        else:
            _qk_norm(project(), tab_ref[0], tab_ref[1], store_q)

    @pl.when(j == G_K)
    def _():
        if is_ctx:
            _qk_norm(project(), qkg_ref[1, :, :LANES], None, store_k)
        else:
            _qk_norm(project(), tab_ref[2], tab_ref[3], store_k)

    @pl.when(j == G_U)
    def _():
        ug_ref[...] = jax.nn.gelu(project())

    @pl.when(j == G_VG)
    def _():
        gv = jax.nn.gelu(project())
        mu = jnp.mean(gv, axis=-1, keepdims=True)
        d = gv - mu
        var = jnp.mean(d * d, axis=-1, keepdims=True)
        vn = (d * lax.rsqrt(var + EPS) * lng_ref[...] + lnb_ref[...]).astype(BF16)
        nch = tm // CHUNK
        for g in range(GMLP_GROUPS):
            cols = slice(g * CHUNK, (g + 1) * CHUNK)
            stacked = jnp.concatenate([vn[n * CHUNK:(n + 1) * CHUNK, cols] for n in range(nch)], axis=1)
            mixed = jnp.dot(ws_ref[g], stacked, preferred_element_type=F32)
            for n in range(nch):
                rows = slice(n * CHUNK, (n + 1) * CHUNK)
                m = mixed[:, n * CHUNK:(n + 1) * CHUNK] + bs_ref[:, cols]
                yc_ref[rows, cols] = (ug_ref[rows, cols] * m).astype(BF16)


def _mix_in_call(x, mod, norm_g, w_in, qk_gain_t, rope, ln_g, ln_b, w_s, b_s_t, caches, *, l, tm, seq, row_of_tile):
    t = x.shape[0]
    is_ctx = caches is not None
    bw = BRANCH_WIDTH
    tok = lambda i, j: (i, 0)
    in_specs = [
        pl.BlockSpec(memory_space=pl.ANY),
        pl.BlockSpec((None, None, 1, D_MODEL), lambda i, j: (l, 1, 0, 0)),
        _mod_spec(l, 3, row_of_tile),
        _mod_spec(l, 4, row_of_tile),
        pl.BlockSpec((None, D_MODEL, bw), lambda i, j: (l, 0, j)),
        pl.BlockSpec((None, 2, 1, bw), lambda i, j: (l, 0, 0, 0)),
    ]
    args = [x, norm_g, mod, mod, w_in, qk_gain_t]
    scratch = [pltpu.VMEM((tm, D_MODEL), BF16), pltpu.VMEM((tm, bw), F32),
               pltpu.VMEM((tm, D_MODEL), F32), pltpu.SemaphoreType.DMA((1,))]
    if not is_ctx:
        tiles_per_seq = seq // tm
        in_specs += [pl.BlockSpec((tm, LANES), lambda i, j: (i % tiles_per_seq, 0))] * 2
        args += [rope[0], rope[1]]
        scratch += [pltpu.VMEM((4, tm, LANES), F32)]
    in_specs += [
        pl.BlockSpec((None, 1, bw), lambda i, j: (l, 0, 0)),
        pl.BlockSpec((None, 1, bw), lambda i, j: (l, 0, 0)),
        pl.BlockSpec((None, GMLP_GROUPS, CHUNK, CHUNK), lambda i, j: (l, 0, 0, 0)),
        pl.BlockSpec((None, CHUNK, bw), lambda i, j: (l, 0, 0)),
    ]
    args += [ln_g, ln_b, w_s, b_s_t]
    act = jax.ShapeDtypeStruct((t, bw), BF16)
    out_shape = [jax.ShapeDtypeStruct((t, N_SLOTS * bw), BF16), act, act, act]
    out_specs = [pl.BlockSpec((tm, bw), lambda i, j: (i, _slab_slot(j)))] + [pl.BlockSpec((tm, bw), tok)] * 3
    aliases = {}
    if is_ctx:
        nb = tm // seq
        cache_spec = pl.BlockSpec((nb, None, seq, bw), lambda i, j: (i, l, 0, 0))
        in_specs += [pl.BlockSpec(memory_space=pl.ANY)] * 2
        aliases = {len(args): len(out_shape), len(args) + 1: len(out_shape) + 1}
        args += list(caches)
        out_shape += [jax.ShapeDtypeStruct(c.shape, c.dtype) for c in caches]
        out_specs += [cache_spec, cache_spec]
    return pl.pallas_call(
        functools.partial(_mix_in_kernel, tm=tm, seq=seq, is_ctx=is_ctx),
        grid=(t // tm, N_IN_GROUPS),
        in_specs=in_specs,
        out_specs=out_specs,
        out_shape=out_shape,
        input_output_aliases=aliases,
        scratch_shapes=scratch,
        compiler_params=_params(("arbitrary", "arbitrary")),
        name="mix_in_ctx" if is_ctx else "mix_in_lat",
    )(*args)


def _lambda_init(l):
    return 0.8 - 0.6 * math.exp(-0.3 * l)


def _lambda_full(lam_ref, l):
    lp = lam_ref[...]
    a = jnp.sum(lp[0:1] * lp[1:2], axis=-1, keepdims=True)
    b = jnp.sum(lp[2:3] * lp[3:4], axis=-1, keepdims=True)
    return jnp.exp(a) - jnp.exp(b) + _lambda_init(l)


def _softmax_times_values(qm, k, vaug):
    s = lax.dot_general(qm, k, (((1,), (1,)), ((), ())), preferred_element_type=F32)
    e = jnp.exp(s - jnp.max(s, axis=-1, keepdims=True)).astype(BF16)
    pv = jnp.dot(e, vaug, preferred_element_type=F32)
    return pv[:, :DV] / pv[:, DV:]


def _attend_head(qh, kh, vaug, lam_full, subg, l):
    lane = lax.broadcasted_iota(jnp.int32, qh.shape, 1)
    zero = jnp.zeros_like(qh)
    o0 = _softmax_times_values(jnp.where(lane < DQK, qh, zero), kh, vaug)
    o1 = _softmax_times_values(jnp.where(lane < DQK, zero, qh), kh, vaug)
    o = o0 - lam_full * o1
    ms = jnp.mean(o * o, axis=-1, keepdims=True)
    return (o * lax.rsqrt(ms + EPS) * subg) * (1.0 - _lambda_init(l))


def _attn_ctx_kernel(q_ref, k_ref, v_ref, lam_ref, subg_ref, o_ref, *, l):
    lam_full = _lambda_full(lam_ref, l)
    ones = jnp.ones((k_ref.shape[0], DV), BF16)
    for h in range(N_HEADS):
        cols = slice(h * DV, (h + 1) * DV)
        vaug = jnp.concatenate([v_ref[:, cols], ones], axis=1)
        o_ref[:, cols] = _attend_head(q_ref[:, cols], k_ref[:, cols], vaug, lam_full, subg_ref[...], l).astype(BF16)


def _attn_ctx_call(q, k, slab, lam, subln_g, *, l, seq):
    t = q.shape[0]
    blk = pl.BlockSpec((seq, BRANCH_WIDTH), lambda b: (b, 0))
    return pl.pallas_call(
        functools.partial(_attn_ctx_kernel, l=l),
        grid=(t // seq,),
        in_specs=[blk, blk, pl.BlockSpec((seq, BRANCH_WIDTH), lambda b: (b, SLOT_V)),
                  pl.BlockSpec((None, 4, DQK), lambda b: (l, 0, 0)),
                  pl.BlockSpec((None, 1, DV), lambda b: (l, 0, 0))],
        out_specs=blk,
        out_shape=jax.ShapeDtypeStruct((t, BRANCH_WIDTH), BF16),
        compiler_params=_params(("parallel",)),
        name="attn_ctx",
    )(q, k, slab, lam, subln_g)


def _attn_lat_kernel(q_ref, k_ref, v_ref, ck_ref, cv_ref, lam_ref, subg_ref, o_ref, kall_ref, vaug_ref, *, l, past):
    @pl.when(pl.program_id(1) == 0)
    def _():
        kall_ref[:past] = ck_ref[...].astype(BF16)
        kall_ref[past:] = k_ref[...]
        for h in range(N_HEADS):
            cols = slice(h * DV, (h + 1) * DV)
            vaug_ref[h, :past, :DV] = cv_ref[:, cols].astype(BF16)
            vaug_ref[h, past:, :DV] = v_ref[:, cols]
            vaug_ref[h, :, DV:] = jnp.ones((vaug_ref.shape[1], DV), BF16)

    lam_full = _lambda_full(lam_ref, l)
    for h in range(N_HEADS):
        cols = slice(h * DV, (h + 1) * DV)
        o_ref[:, cols] = _attend_head(q_ref[:, cols], kall_ref[:, cols], vaug_ref[h], lam_full, subg_ref[...], l).astype(BF16)


def _attn_lat_call(q, k, slab, cache_k, cache_v, lam, subln_g, *, l, seq, tq):
    t = q.shape[0]
    past = cache_k.shape[2]
    nq = seq // tq
    bw = BRANCH_WIDTH
    cache = pl.BlockSpec((None, None, past, bw), lambda b, i: (b, l, 0, 0))
    return pl.pallas_call(
        functools.partial(_attn_lat_kernel, l=l, past=past),
        grid=(t // seq, nq),
        in_specs=[pl.BlockSpec((tq, bw), lambda b, i: (b * nq + i, 0)),
                  pl.BlockSpec((seq, bw), lambda b, i: (b, 0)),
                  pl.BlockSpec((seq, bw), lambda b, i: (b, SLOT_V)),
                  cache, cache,
                  pl.BlockSpec((None, 4, DQK), lambda b, i: (l, 0, 0)),
                  pl.BlockSpec((None, 1, DV), lambda b, i: (l, 0, 0))],
        out_specs=pl.BlockSpec((tq, bw), lambda b, i: (b * nq + i, 0)),
        out_shape=jax.ShapeDtypeStruct((t, bw), BF16),
        scratch_shapes=[pltpu.VMEM((past + seq, bw), BF16), pltpu.VMEM((N_HEADS, past + seq, 2 * DV), BF16)],
        compiler_params=_params(("parallel", "arbitrary")),
        name="attn_lat",
    )(q, k, slab, cache_k, cache_v, lam, subln_g)


def _conv_kernel(xa_ref, gc_ref, gb_ref, wc_ref, o_ref, *, seq):
    p = xa_ref[...].astype(F32) * gc_ref[...].astype(F32)
    rows = p.shape[0]
    pos = lax.broadcasted_iota(jnp.int32, p.shape, 0) % seq
    prev = jnp.where(pos == 0, 0.0, pltpu.roll(p, 1, axis=0))
    nxt = jnp.where(pos == seq - 1, 0.0, pltpu.roll(p, rows - 1, axis=0))
    conv = wc_ref[0:1, :] * prev + wc_ref[1:2, :] * p + wc_ref[2:3, :] * nxt
    o_ref[...] = (gb_ref[...].astype(F32) * conv).astype(BF16)


def _conv_call(slab, w_conv, *, l, rows, tc, seq):
    t = slab.shape[0]
    bw = BRANCH_WIDTH
    nc = bw // tc
    assert rows % seq == 0 and t % rows == 0

    def col(slot):
        return pl.BlockSpec((rows, tc), lambda i, c: (i, slot * nc + c))

    return pl.pallas_call(
        functools.partial(_conv_kernel, seq=seq),
        grid=(t // rows, nc),
        in_specs=[col(SLOT_XA), col(SLOT_GC), col(SLOT_GB), pl.BlockSpec((None, 3, tc), lambda i, c: (l, 0, c))],
        out_specs=pl.BlockSpec((rows, tc), lambda i, c: (i, c)),
        out_shape=jax.ShapeDtypeStruct((t, bw), BF16),
        compiler_params=_params(("parallel", "parallel")),
        name="short_conv",
    )(slab, slab, slab, w_conv)


def _mix_out_kernel(x_hbm, gt_ref, ya_ref, yb_ref, yc_ref, za_ref, zb_ref, zc_ref, wa_ref, wb_ref, wcc_ref, wo_ref,
                    o_hbm, res_ref, sem_in, sem_out, *, tm):
    tiles = _ResidualTiles(x_hbm, o_hbm, res_ref, sem_in, sem_out, tm)
    tiles.arrive()
    tiles.prefetch_next()
    acc = res_ref.at[tiles.slot]
    ya = jnp.dot(ya_ref[...], wa_ref[...], preferred_element_type=F32)
    yb = jnp.dot(yb_ref[...], wb_ref[...], preferred_element_type=F32)
    yc = jnp.dot(yc_ref[...], wcc_ref[...], preferred_element_type=F32)
    merged = (_sigmoid(za_ref[...].astype(F32)) * ya + _sigmoid(zb_ref[...].astype(F32)) * yb
              + _sigmoid(zc_ref[...].astype(F32)) * yc)
    acc[...] += gt_ref[...] * jnp.dot(merged.astype(BF16), wo_ref[...], preferred_element_type=F32)
    tiles.depart()


def _mix_out_call(x, mod, slab, ya, yb, yc, w_branch, w_o, *, l, tm, tn, row_of_tile):
    t = x.shape[0]
    bw = BRANCH_WIDTH
    nn = D_MODEL // tn
    assert nn >= 2 and t % tm == 0
    gate0 = SLOT_GATES * bw // tn

    def gate(branch):
        return pl.BlockSpec((tm, tn), lambda i, n: (i, gate0 + branch * nn + n))

    def branch_w(branch):
        return pl.BlockSpec((None, None, bw, tn), lambda i, n: (l, branch, 0, n))

    act = pl.BlockSpec((tm, bw), lambda i, n: (i, 0))
    return pl.pallas_call(
        functools.partial(_mix_out_kernel, tm=tm),
        grid=(t // tm, nn),
        in_specs=[
            pl.BlockSpec(memory_space=pl.ANY),
            _mod_spec(l, 5, row_of_tile),
            act, act, act,
            gate(0), gate(1), gate(2),
            branch_w(0), branch_w(1), branch_w(2),
            pl.BlockSpec((None, tn, D_MODEL), lambda i, n: (l, n, 0)),
        ],
        out_specs=pl.BlockSpec(memory_space=pl.ANY),
        out_shape=jax.ShapeDtypeStruct((t, D_MODEL), F32),
        input_output_aliases={0: 0},
        scratch_shapes=[pltpu.VMEM((2, tm, D_MODEL), F32), pltpu.SemaphoreType.DMA((2,)), pltpu.SemaphoreType.DMA((2,))],
        compiler_params=_params(("arbitrary", "arbitrary")),
        name="mix_out",
    )(x, mod, ya, yb, yc, slab, slab, slab, w_branch, w_branch, w_branch, w_o)


def _rope_tables(seq):
    rows = seq // GRID_W
    row = jnp.repeat(jnp.arange(rows, dtype=F32), GRID_W)
    col = jnp.tile(jnp.arange(GRID_W, dtype=F32), rows)
    inv = ROPE_THETA ** (-jnp.arange(0, AXIS_DIM, 2, dtype=F32) / AXIS_DIM)
    ang_r, ang_c = row[:, None] * inv, col[:, None] * inv
    cos = jnp.concatenate([jnp.cos(ang_r)] * 2 + [jnp.cos(ang_c)] * 2, axis=1)
    sin = jnp.concatenate([-jnp.sin(ang_r), jnp.sin(ang_r), -jnp.sin(ang_c), jnp.sin(ang_c)], axis=1)
    return jnp.tile(cos, (1, LANES // DQK)), jnp.tile(sin, (1, LANES // DQK))


def kernel(x_prompt, x_sample, c, c_ctx, cache_k, cache_v, w_mod, b_mod, norm_g, w_ffn_in, w_ffn_out,
           w_in, w_conv, qk_gain, lam, subln_g, gmlp_ln_g, gmlp_ln_b, w_s, b_s, w_branch, w_o):
    batch, seq, _ = x_prompt.shape
    dec_batch, dec_seq, _ = x_sample.shape
    depth = w_mod.shape[0]
    past = cache_k.shape[2]
    bw = BRANCH_WIDTH
    assert dec_batch + 1 <= MOD_ROWS and seq % CHUNK == 0 and dec_seq % CHUNK == 0

    def ctx_tile(want):
        nb = max(1, want // seq)
        while batch % nb:
            nb -= 1
        return nb * seq

    tm_in = {"ctx": ctx_tile(512), "lat": min(1024, dec_seq)}
    tm_res = {"ctx": ctx_tile(1024), "lat": min(1024, dec_seq)}
    conv_rows = {"ctx": ctx_tile(2048), "lat": dec_seq}
    tf = 512
    tn_out = 512
    tc_conv = 256
    tq = 256

    cvec = jnp.zeros((MOD_ROWS, D_MODEL), F32).at[0].set(c_ctx).at[1:1 + dec_batch].set(c)
    mod = _mod_call(cvec, w_mod, b_mod).reshape(depth, MOD_ROWS, N_MOD, 1, D_MODEL)

    w_ffn_in_b = w_ffn_in.astype(BF16)
    w_ffn_out_b = w_ffn_out.astype(BF16)
    w_in_b = w_in.astype(BF16)
    w_branch_b = w_branch.astype(BF16)
    w_o_b = w_o.astype(BF16)
    w_s_b = w_s.astype(BF16)
    norm_g4 = norm_g.reshape(depth, 3, 1, D_MODEL)
    qk_gain_t = jnp.tile(qk_gain, (1, 1, bw // DQK)).reshape(depth, 2, 1, bw)
    ln_g3 = gmlp_ln_g.reshape(depth, 1, bw)
    ln_b3 = gmlp_ln_b.reshape(depth, 1, bw)
    b_s_t = jnp.repeat(jnp.swapaxes(b_s, 1, 2), bw // GMLP_GROUPS, axis=2)
    subg3 = subln_g.reshape(depth, 1, DV)
    rope = _rope_tables(dec_seq)
    cache_k4 = cache_k.reshape(dec_batch, depth, past, bw)
    cache_v4 = cache_v.reshape(dec_batch, depth, past, bw)

    def row_map(path, tm):
        if path == "ctx":
            return lambda i: 0
        tiles_per_seq = dec_seq // tm
        return lambda i: 1 + i // tiles_per_seq

    xp = x_prompt.reshape(batch * seq, D_MODEL)
    xs = x_sample.reshape(dec_batch * dec_seq, D_MODEL)
    new_caches = (jnp.zeros((batch, depth, seq, bw), F32), jnp.zeros((batch, depth, seq, bw), F32))

    for l in range(depth):
        for path in ("ctx", "lat"):
            is_ctx = path == "ctx"
            x = xp if is_ctx else xs
            sq = seq if is_ctx else dec_seq
            tmi, tmr = tm_in[path], tm_res[path]
            ffn = functools.partial(_ffn_call, mod=mod, norm_g=norm_g4, w_in=w_ffn_in_b, w_out=w_ffn_out_b,
                                    l=l, tm=tmr, tf=tf, row_of_tile=row_map(path, tmr))
            x = ffn(x, s=0, in_place=l > 0)
            outs = _mix_in_call(x, mod, norm_g4, w_in_b, qk_gain_t, None if is_ctx else rope, ln_g3, ln_b3, w_s_b, b_s_t,
                                new_caches if is_ctx else None, l=l, tm=tmi, seq=sq, row_of_tile=row_map(path, tmi))
            slab, q, k, yc = outs[:4]
            ya = _conv_call(slab, w_conv, l=l, rows=conv_rows[path], tc=tc_conv, seq=sq)
            if is_ctx:
                new_caches = (outs[4], outs[5])
                yb = _attn_ctx_call(q, k, slab, lam, subg3, l=l, seq=sq)
            else:
                yb = _attn_lat_call(q, k, slab, cache_k4, cache_v4, lam, subg3, l=l, seq=sq, tq=tq)
            x = _mix_out_call(x, mod, slab, ya, yb, yc, w_branch_b, w_o_b,
                              l=l, tm=tmr, tn=tn_out, row_of_tile=row_map(path, tmr))
            x = ffn(x, s=1)
            if is_ctx:
                xp = x
            else:
                xs = x

    return (xp.reshape(batch, seq, D_MODEL), xs.reshape(dec_batch, dec_seq, D_MODEL),
            new_caches[0].reshape(batch, depth, seq, N_HEADS, 2, DQK), new_caches[1].reshape(batch, depth, seq, N_HEADS, DV))
```

```python
import functools
import math

import jax
import jax.numpy as jnp
from jax import lax
from jax.experimental import pallas as pl
from jax.experimental.pallas import tpu as pltpu

D_MODEL = 2048
GRID_W = 64
N_MOD = 9
D_FF = 5632
BRANCH_WIDTH = D_MODEL // 2
N_HEADS = 8
DQK = 64
DV = 2 * DQK
GMLP_GROUPS = 8
CHUNK = 128
N_BRANCH = 3
ROPE_THETA = 10000.0
AXIS_DIM = DQK // 2
EPS = 1e-6
N_IN_GROUPS = 8 + 2 * N_BRANCH
LANES = 128
MXU_DIM = 256
MOD_ROWS = 16
HALO = 16
VMEM_LIMIT = 56 * 1024 * 1024

G_XA, G_GB, G_GC, G_Q, G_K, G_V, G_U, G_VG, G_GATES = range(9)
SLOT_XA, SLOT_GB, SLOT_GC, SLOT_V, SLOT_GATES = 0, 1, 2, 3, 4
N_SLOTS = SLOT_GATES + 2 * N_BRANCH

F32 = jnp.float32
BF16 = jnp.bfloat16


def _slab_slot(j):
    return jnp.where(j < G_Q, j, jnp.where(j <= G_V, SLOT_V, jnp.where(j <= G_GATES, SLOT_GATES, j - G_GATES + SLOT_GATES)))


def _params(semantics):
    return pltpu.CompilerParams(dimension_semantics=semantics, vmem_limit_bytes=VMEM_LIMIT)


def _sigmoid(t):
    return 0.5 * jnp.tanh(0.5 * t) + 0.5


def _modulated_norm(x, gain, shift, scale):
    ms = jnp.mean(x * x, axis=-1, keepdims=True)
    return (x * lax.rsqrt(ms + EPS)) * (gain * (1.0 + scale)) + shift


def _mod_kernel(c_ref, w_ref, b_ref, o_ref):
    c = c_ref[...]
    a = (c * _sigmoid(c)).astype(BF16)
    o_ref[...] = jnp.dot(a, w_ref[...].astype(BF16), preferred_element_type=F32) + b_ref[...]


def _mod_call(cvec, w_mod, b_mod):
    depth = w_mod.shape[0]
    tn = 1024
    nn = N_MOD * D_MODEL // tn
    return pl.pallas_call(
        _mod_kernel,
        grid=(depth, nn),
        in_specs=[
            pl.BlockSpec((MOD_ROWS, D_MODEL), lambda l, n: (0, 0)),
            pl.BlockSpec((None, D_MODEL, tn), lambda l, n: (l, 0, n)),
            pl.BlockSpec((None, 1, tn), lambda l, n: (l, 0, n)),
        ],
        out_specs=pl.BlockSpec((None, MOD_ROWS, tn), lambda l, n: (l, 0, n)),
        out_shape=jax.ShapeDtypeStruct((depth, MOD_ROWS, N_MOD * D_MODEL), F32),
        compiler_params=_params(("parallel", "parallel")),
        name="adaln_mod",
    )(cvec, w_mod, b_mod.reshape(depth, 1, N_MOD * D_MODEL))


def _mod_spec(l, j, row_of_tile):
    return pl.BlockSpec((None, None, None, 1, D_MODEL), lambda i, n: (l, row_of_tile(i), j, 0, 0))


class _ResidualTiles:
    def __init__(self, x_hbm, o_hbm, buf_ref, sem_in, sem_out, tm):
        self.x_hbm, self.o_hbm, self.buf, self.sem_in, self.sem_out, self.tm = x_hbm, o_hbm, buf_ref, sem_in, sem_out, tm
        self.i, self.f = pl.program_id(0), pl.program_id(1)
        self.ntiles, self.nsteps = pl.num_programs(0), pl.num_programs(1)
        self.slot = self.i % 2

    def _rows(self, tile):
        return pl.ds(pl.multiple_of(tile * self.tm, self.tm), self.tm)

    def _load(self, tile, slot):
        return pltpu.make_async_copy(self.x_hbm.at[self._rows(tile)], self.buf.at[slot], self.sem_in.at[slot])

    def _store(self, tile, slot):
        return pltpu.make_async_copy(self.buf.at[slot], self.o_hbm.at[self._rows(tile)], self.sem_out.at[slot])

    def arrive(self):
        @pl.when((self.i == 0) & (self.f == 0))
        def _():
            self._load(0, 0).start()

        @pl.when(self.f == 0)
        def _():
            self._load(self.i, self.slot).wait()

    def prefetch_next(self):
        @pl.when((self.f == 1) & (self.i + 1 < self.ntiles))
        def _():
            @pl.when(self.i >= 1)
            def _():
                self._store(self.i - 1, 1 - self.slot).wait()
            self._load(self.i + 1, 1 - self.slot).start()

    def depart(self):
        last = self.f == self.nsteps - 1

        @pl.when(last)
        def _():
            self._store(self.i, self.slot).start()

        @pl.when(last & (self.i == self.ntiles - 1))
        def _():
            @pl.when(self.ntiles >= 2)
            def _():
                self._store(self.i - 1, 1 - self.slot).wait()
            self._store(self.i, self.slot).wait()


def _ffn_kernel(x_hbm, g_ref, sh_ref, sc_ref, gt_ref, wa_ref, wg_ref, wo_ref, o_hbm, res_ref, h_ref, sem_in, sem_out, *, tm):
    tiles = _ResidualTiles(x_hbm, o_hbm, res_ref, sem_in, sem_out, tm)
    tiles.arrive()
    acc = res_ref.at[tiles.slot]

    def accumulate(h):
        a = jnp.dot(h, wa_ref[...], preferred_element_type=F32)
        g = jnp.dot(h, wg_ref[...], preferred_element_type=F32)
        act = (a * _sigmoid(a) * g).astype(BF16)
        acc[...] += (0.5 * gt_ref[...]) * jnp.dot(act, wo_ref[...], preferred_element_type=F32)

    @pl.when(pl.program_id(1) == 0)
    def _():
        h = _modulated_norm(acc[...], g_ref[...], sh_ref[...], sc_ref[...]).astype(BF16)
        h_ref[...] = h
        accumulate(h)

    tiles.prefetch_next()

    @pl.when(pl.program_id(1) > 0)
    def _():
        accumulate(h_ref[...])

    tiles.depart()


def _ffn_call(x, mod, norm_g, w_in, w_out, *, l, s, tm, tf, row_of_tile, in_place=True):
    t = x.shape[0]
    nf = D_FF // tf
    assert nf >= 2 and t % tm == 0
    mod_base = 0 if s == 0 else 6
    norm_idx = 0 if s == 0 else 2
    return pl.pallas_call(
        functools.partial(_ffn_kernel, tm=tm),
        grid=(t // tm, nf),
        in_specs=[
            pl.BlockSpec(memory_space=pl.ANY),
            pl.BlockSpec((None, None, 1, D_MODEL), lambda i, f: (l, norm_idx, 0, 0)),
            _mod_spec(l, mod_base + 0, row_of_tile),
            _mod_spec(l, mod_base + 1, row_of_tile),
            _mod_spec(l, mod_base + 2, row_of_tile),
            pl.BlockSpec((None, None, D_MODEL, tf), lambda i, f: (l, s, 0, f)),
            pl.BlockSpec((None, None, D_MODEL, tf), lambda i, f: (l, s, 0, nf + f)),
            pl.BlockSpec((None, None, tf, D_MODEL), lambda i, f: (l, s, f, 0)),
        ],
        out_specs=pl.BlockSpec(memory_space=pl.ANY),
        out_shape=jax.ShapeDtypeStruct((t, D_MODEL), F32),
        input_output_aliases={0: 0} if in_place else {},
        scratch_shapes=[pltpu.VMEM((2, tm, D_MODEL), F32), pltpu.VMEM((tm, D_MODEL), BF16),
                        pltpu.SemaphoreType.DMA((2,)), pltpu.SemaphoreType.DMA((2,))],
        compiler_params=_params(("arbitrary", "arbitrary")),
        name=f"ffn{s}",
    )(x, norm_g, mod, mod, mod, w_in, w_in, w_out)


def _partner(y):
    lane = lax.broadcasted_iota(jnp.int32, y.shape, 1)
    ahead = pltpu.roll(y, LANES - AXIS_DIM // 2, axis=1)
    behind = pltpu.roll(y, AXIS_DIM // 2, axis=1)
    return jnp.where(lane % AXIS_DIM < AXIS_DIM // 2, ahead, behind)


def _segment_ones():
    r = lax.broadcasted_iota(jnp.int32, (MXU_DIM, MXU_DIM), 0)
    c = lax.broadcasted_iota(jnp.int32, (MXU_DIM, MXU_DIM), 1)
    return jnp.where(r // DQK == c // DQK, 1.0, 0.0).astype(BF16)


def _qk_norm(z, cos_g, sin_g, store):
    seg = _segment_ones()
    for pair in range(BRANCH_WIDTH // MXU_DIM):
        zp = z[:, pair * MXU_DIM:(pair + 1) * MXU_DIM]
        ss = jnp.dot((zp * zp).astype(BF16), seg, preferred_element_type=F32)
        r = lax.rsqrt(ss * (1.0 / DQK) + EPS)
        for half in range(MXU_DIM // LANES):
            zc = zp[:, half * LANES:(half + 1) * LANES]
            y = zc * cos_g
            if sin_g is not None:
                y = y + _partner(zc) * sin_g
            store(pair * (MXU_DIM // LANES) + half, y * r[:, half * LANES:(half + 1) * LANES])


def _mix_in_kernel(*refs, tm, seq, is_ctx):
    it = iter(refs)
    x_ref, g_ref, sh_ref, sc_ref, w_ref, qkg_ref = (next(it) for _ in range(6))
    cos_ref = sin_ref = None
    if not is_ctx:
        cos_ref, sin_ref = next(it), next(it)
    lng_ref, lnb_ref, ws_ref, bs_ref = (next(it) for _ in range(4))
    if is_ctx:
        next(it), next(it)
    slab_ref, q_ref, k_ref, yc_ref = (next(it) for _ in range(4))
    kc_ref = vc_ref = None
    if is_ctx:
        kc_ref, vc_ref = next(it), next(it)
    h_ref, ug_ref, xbuf_ref, xsem = (next(it) for _ in range(4))
    tab_ref = None if is_ctx else next(it)

    i, j = pl.program_id(0), pl.program_id(1)
    q_scale = DQK ** -0.5 * math.log2(math.e)

    def load_x(tile):
        rows = pl.ds(pl.multiple_of(tile * tm, tm), tm)
        return pltpu.make_async_copy(x_ref.at[rows], xbuf_ref, xsem.at[0])

    @pl.when((i == 0) & (j == 0))
    def _():
        load_x(0).start()

    @pl.when((j == 1) & (i + 1 < pl.num_programs(0)))
    def _():
        load_x(i + 1).start()

    def project():
        return jnp.dot(h_ref[...], w_ref[...], preferred_element_type=F32)

    @pl.when(j == G_XA)
    def _():
        load_x(i).wait()
        h = _modulated_norm(xbuf_ref[...], g_ref[...], sh_ref[...], sc_ref[...]).astype(BF16)
        h_ref[...] = h
        if not is_ctx:
            cos, sin = cos_ref[...], sin_ref[...]
            gq = jnp.broadcast_to(qkg_ref[0, :, :LANES], cos.shape) * q_scale
            gk = jnp.broadcast_to(qkg_ref[1, :, :LANES], cos.shape)
            tab_ref[0] = cos * gq
            tab_ref[1] = sin * _partner(gq)
            tab_ref[2] = cos * gk
            tab_ref[3] = sin * _partner(gk)
        slab_ref[...] = jnp.dot(h, w_ref[...], preferred_element_type=F32).astype(BF16)

    cast_only = (j == G_GB) | (j == G_GC) | (j >= G_GATES)
    if not is_ctx:
        cast_only = cast_only | (j == G_V)

    @pl.when(cast_only)
    def _():
        slab_ref[...] = project().astype(BF16)

    if is_ctx:
        @pl.when(j == G_V)
        def _():
            z = project()
            slab_ref[...] = z.astype(BF16)
            for b in range(tm // seq):
                vc_ref[b] = z[b * seq:(b + 1) * seq]

    def store_q(c, y):
        q_ref[:, c * LANES:(c + 1) * LANES] = y.astype(BF16)

    def store_k(c, y):
        k_ref[:, c * LANES:(c + 1) * LANES] = y.astype(BF16)
        if is_ctx:
            for b in range(tm // seq):
                kc_ref[b, :, c * LANES:(c + 1) * LANES] = y[b * seq:(b + 1) * seq]

    @pl.when(j == G_Q)
    def _():
        if is_ctx:
            _qk_norm(project(), qkg_ref[0, :, :LANES] * q_scale, None, store_q)
        else:
            _qk_norm(project(), tab_ref[0], tab_ref[1], store_q)

    @pl.when(j == G_K)
    def _():
        if is_ctx:
            _qk_norm(project(), qkg_ref[1, :, :LANES], None, store_k)
        else:
            _qk_norm(project(), tab_ref[2], tab_ref[3], store_k)

    @pl.when(j == G_U)
    def _():
        ug_ref[...] = jax.nn.gelu(project())

    @pl.when(j == G_VG)
    def _():
        gv = jax.nn.gelu(project())
        mu = jnp.mean(gv, axis=-1, keepdims=True)
        d = gv - mu
        var = jnp.mean(d * d, axis=-1, keepdims=True)
        vn = (d * lax.rsqrt(var + EPS) * lng_ref[...] + lnb_ref[...]).astype(BF16)
        nch = tm // CHUNK
        for g in range(GMLP_GROUPS):
            cols = slice(g * CHUNK, (g + 1) * CHUNK)
            stacked = jnp.concatenate([vn[n * CHUNK:(n + 1) * CHUNK, cols] for n in range(nch)], axis=1)
            mixed = jnp.dot(ws_ref[g], stacked, preferred_element_type=F32)
            for n in range(nch):
                rows = slice(n * CHUNK, (n + 1) * CHUNK)
                m = mixed[:, n * CHUNK:(n + 1) * CHUNK] + bs_ref[:, cols]
                yc_ref[rows, cols] = (ug_ref[rows, cols] * m).astype(BF16)


def _mix_in_call(x, mod, norm_g, w_in, qk_gain_t, rope, ln_g, ln_b, w_s, b_s_t, caches, *, l, tm, seq, row_of_tile):
    t = x.shape[0]
    is_ctx = caches is not None
    bw = BRANCH_WIDTH
    tok = lambda i, j: (i, 0)
    in_specs = [
        pl.BlockSpec(memory_space=pl.ANY),
        pl.BlockSpec((None, None, 1, D_MODEL), lambda i, j: (l, 1, 0, 0)),
        _mod_spec(l, 3, row_of_tile),
        _mod_spec(l, 4, row_of_tile),
        pl.BlockSpec((None, D_MODEL, bw), lambda i, j: (l, 0, j)),
        pl.BlockSpec((None, 2, 1, bw), lambda i, j: (l, 0, 0, 0)),
    ]
    args = [x, norm_g, mod, mod, w_in, qk_gain_t]
    scratch = [pltpu.VMEM((tm, D_MODEL), BF16), pltpu.VMEM((tm, bw), F32),
               pltpu.VMEM((tm, D_MODEL), F32), pltpu.SemaphoreType.DMA((1,))]
    if not is_ctx:
        tiles_per_seq = seq // tm
        in_specs += [pl.BlockSpec((tm, LANES), lambda i, j: (i % tiles_per_seq, 0))] * 2
        args += [rope[0], rope[1]]
        scratch += [pltpu.VMEM((4, tm, LANES), F32)]
    in_specs += [
        pl.BlockSpec((None, 1, bw), lambda i, j: (l, 0, 0)),
        pl.BlockSpec((None, 1, bw), lambda i, j: (l, 0, 0)),
        pl.BlockSpec((None, GMLP_GROUPS, CHUNK, CHUNK), lambda i, j: (l, 0, 0, 0)),
        pl.BlockSpec((None, CHUNK, bw), lambda i, j: (l, 0, 0)),
    ]
    args += [ln_g, ln_b, w_s, b_s_t]
    act = jax.ShapeDtypeStruct((t, bw), BF16)
    out_shape = [jax.ShapeDtypeStruct((t, N_SLOTS * bw), BF16), act, act, act]
    out_specs = [pl.BlockSpec((tm, bw), lambda i, j: (i, _slab_slot(j)))] + [pl.BlockSpec((tm, bw), tok)] * 3
    aliases = {}
    if is_ctx:
        nb = tm // seq
        cache_spec = pl.BlockSpec((nb, None, seq, bw), lambda i, j: (i, l, 0, 0))
        in_specs += [pl.BlockSpec(memory_space=pl.ANY)] * 2
        aliases = {len(args): len(out_shape), len(args) + 1: len(out_shape) + 1}
        args += list(caches)
        out_shape += [jax.ShapeDtypeStruct(c.shape, c.dtype) for c in caches]
        out_specs += [cache_spec, cache_spec]
    return pl.pallas_call(
        functools.partial(_mix_in_kernel, tm=tm, seq=seq, is_ctx=is_ctx),
        grid=(t // tm, N_IN_GROUPS),
        in_specs=in_specs,
        out_specs=out_specs,
        out_shape=out_shape,
        input_output_aliases=aliases,
        scratch_shapes=scratch,
        compiler_params=_params(("arbitrary", "arbitrary")),
        name="mix_in_ctx" if is_ctx else "mix_in_lat",
    )(*args)


def _lambda_init(l):
    return 0.8 - 0.6 * math.exp(-0.3 * l)


def _lambda_full(lam_ref, l):
    lp = lam_ref[...]
    a = jnp.sum(lp[0:1] * lp[1:2], axis=-1, keepdims=True)
    b = jnp.sum(lp[2:3] * lp[3:4], axis=-1, keepdims=True)
    return jnp.exp(a) - jnp.exp(b) + _lambda_init(l)


def _softmax_times_values(qm, k, vaug):
    s = lax.dot_general(qm, k, (((1,), (1,)), ((), ())), preferred_element_type=F32)
    e = jnp.exp2(s - jnp.max(s, axis=-1, keepdims=True)).astype(BF16)
    pv = jnp.dot(e, vaug, preferred_element_type=F32)
    return pv[:, :DV] / pv[:, DV:]


def _attend_head(qh, kh, vaug, lam_full, subg, l):
    lane = lax.broadcasted_iota(jnp.int32, qh.shape, 1)
    zero = jnp.zeros_like(qh)
    o0 = _softmax_times_values(jnp.where(lane < DQK, qh, zero), kh, vaug)
    o1 = _softmax_times_values(jnp.where(lane < DQK, zero, qh), kh, vaug)
    o = o0 - lam_full * o1
    ms = jnp.mean(o * o, axis=-1, keepdims=True)
    return (o * lax.rsqrt(ms + EPS) * subg) * (1.0 - _lambda_init(l))


def _attn_ctx_kernel(q_ref, k_ref, v_ref, lam_ref, subg_ref, o_ref, *, l):
    lam_full = _lambda_full(lam_ref, l)
    ones = jnp.ones((k_ref.shape[0], DV), BF16)
    for h in range(N_HEADS):
        cols = slice(h * DV, (h + 1) * DV)
        vaug = jnp.concatenate([v_ref[:, cols], ones], axis=1)
        o_ref[:, cols] = _attend_head(q_ref[:, cols], k_ref[:, cols], vaug, lam_full, subg_ref[...], l).astype(BF16)


def _attn_ctx_call(q, k, slab, lam, subln_g, *, l, seq):
    t = q.shape[0]
    blk = pl.BlockSpec((seq, BRANCH_WIDTH), lambda b: (b, 0))
    return pl.pallas_call(
        functools.partial(_attn_ctx_kernel, l=l),
        grid=(t // seq,),
        in_specs=[blk, blk, pl.BlockSpec((seq, BRANCH_WIDTH), lambda b: (b, SLOT_V)),
                  pl.BlockSpec((None, 4, DQK), lambda b: (l, 0, 0)),
                  pl.BlockSpec((None, 1, DV), lambda b: (l, 0, 0))],
        out_specs=blk,
        out_shape=jax.ShapeDtypeStruct((t, BRANCH_WIDTH), BF16),
        compiler_params=_params(("parallel",)),
        name="attn_ctx",
    )(q, k, slab, lam, subln_g)


def _attn_lat_kernel(q_ref, k_ref, v_ref, ck_ref, cv_ref, lam_ref, subg_ref, o_ref, kall_ref, vaug_ref, *, l, past):
    @pl.when(pl.program_id(1) == 0)
    def _():
        kall_ref[:past] = ck_ref[...].astype(BF16)
        kall_ref[past:] = k_ref[...]
        for h in range(N_HEADS):
            cols = slice(h * DV, (h + 1) * DV)
            vaug_ref[h, :past, :DV] = cv_ref[:, cols].astype(BF16)
            vaug_ref[h, past:, :DV] = v_ref[:, cols]
            vaug_ref[h, :, DV:] = jnp.ones((vaug_ref.shape[1], DV), BF16)

    lam_full = _lambda_full(lam_ref, l)
    for h in range(N_HEADS):
        cols = slice(h * DV, (h + 1) * DV)
        o_ref[:, cols] = _attend_head(q_ref[:, cols], kall_ref[:, cols], vaug_ref[h], lam_full, subg_ref[...], l).astype(BF16)


def _attn_lat_call(q, k, slab, cache_k, cache_v, lam, subln_g, *, l, seq, tq):
    t = q.shape[0]
    past = cache_k.shape[2]
    nq = seq // tq
    bw = BRANCH_WIDTH
    cache = pl.BlockSpec((None, None, past, bw), lambda b, i: (b, l, 0, 0))
    return pl.pallas_call(
        functools.partial(_attn_lat_kernel, l=l, past=past),
        grid=(t // seq, nq),
        in_specs=[pl.BlockSpec((tq, bw), lambda b, i: (b * nq + i, 0)),
                  pl.BlockSpec((seq, bw), lambda b, i: (b, 0)),
                  pl.BlockSpec((seq, bw), lambda b, i: (b, SLOT_V)),
                  cache, cache,
                  pl.BlockSpec((None, 4, DQK), lambda b, i: (l, 0, 0)),
                  pl.BlockSpec((None, 1, DV), lambda b, i: (l, 0, 0))],
        out_specs=pl.BlockSpec((tq, bw), lambda b, i: (b * nq + i, 0)),
        out_shape=jax.ShapeDtypeStruct((t, bw), BF16),
        scratch_shapes=[pltpu.VMEM((past + seq, bw), BF16), pltpu.VMEM((N_HEADS, past + seq, 2 * DV), BF16)],
        compiler_params=_params(("parallel", "arbitrary")),
        name="attn_lat",
    )(q, k, slab, cache_k, cache_v, lam, subln_g)


def _conv_kernel(xa_ref, gc_ref, gb_ref, wc_ref, o_ref, *, seq):
    p = xa_ref[...].astype(F32) * gc_ref[...].astype(F32)
    rows = p.shape[0]
    pos = lax.broadcasted_iota(jnp.int32, p.shape, 0) % seq
    prev = jnp.where(pos == 0, 0.0, pltpu.roll(p, 1, axis=0))
    nxt = jnp.where(pos == seq - 1, 0.0, pltpu.roll(p, rows - 1, axis=0))
    conv = wc_ref[0:1, :] * prev + wc_ref[1:2, :] * p + wc_ref[2:3, :] * nxt
    o_ref[...] = (gb_ref[...].astype(F32) * conv).astype(BF16)


def _conv_call(slab, w_conv, *, l, rows, tc, seq):
    t = slab.shape[0]
    bw = BRANCH_WIDTH
    nc = bw // tc
    assert rows % seq == 0 and t % rows == 0

    def col(slot):
        return pl.BlockSpec((rows, tc), lambda i, c: (i, slot * nc + c))

    return pl.pallas_call(
        functools.partial(_conv_kernel, seq=seq),
        grid=(t // rows, nc),
        in_specs=[col(SLOT_XA), col(SLOT_GC), col(SLOT_GB), pl.BlockSpec((None, 3, tc), lambda i, c: (l, 0, c))],
        out_specs=pl.BlockSpec((rows, tc), lambda i, c: (i, c)),
        out_shape=jax.ShapeDtypeStruct((t, bw), BF16),
        compiler_params=_params(("parallel", "parallel")),
        name="short_conv",
    )(slab, slab, slab, w_conv)


def _mix_out_kernel(x_hbm, gt_ref, ya_ref, yb_ref, yc_ref, za_ref, zb_ref, zc_ref, wa_ref, wb_ref, wcc_ref, wo_ref,
                    o_hbm, res_ref, sem_in, sem_out, *, tm):
    tiles = _ResidualTiles(x_hbm, o_hbm, res_ref, sem_in, sem_out, tm)
    tiles.arrive()
    tiles.prefetch_next()
    acc = res_ref.at[tiles.slot]
    ya = jnp.dot(ya_ref[...], wa_ref[...], preferred_element_type=F32)
    yb = jnp.dot(yb_ref[...], wb_ref[...], preferred_element_type=F32)
    yc = jnp.dot(yc_ref[...], wcc_ref[...], preferred_element_type=F32)
    merged = (_sigmoid(za_ref[...].astype(F32)) * ya + _sigmoid(zb_ref[...].astype(F32)) * yb
              + _sigmoid(zc_ref[...].astype(F32)) * yc)
    acc[...] += gt_ref[...] * jnp.dot(merged.astype(BF16), wo_ref[...], preferred_element_type=F32)
    tiles.depart()


def _mix_out_call(x, mod, slab, ya, yb, yc, w_branch, w_o, *, l, tm, tn, row_of_tile):
    t = x.shape[0]
    bw = BRANCH_WIDTH
    nn = D_MODEL // tn
    assert nn >= 2 and t % tm == 0
    gate0 = SLOT_GATES * bw // tn

    def gate(branch):
        return pl.BlockSpec((tm, tn), lambda i, n: (i, gate0 + branch * nn + n))

    def branch_w(branch):
        return pl.BlockSpec((None, None, bw, tn), lambda i, n: (l, branch, 0, n))

    act = pl.BlockSpec((tm, bw), lambda i, n: (i, 0))
    return pl.pallas_call(
        functools.partial(_mix_out_kernel, tm=tm),
        grid=(t // tm, nn),
        in_specs=[
            pl.BlockSpec(memory_space=pl.ANY),
            _mod_spec(l, 5, row_of_tile),
            act, act, act,
            gate(0), gate(1), gate(2),
            branch_w(0), branch_w(1), branch_w(2),
            pl.BlockSpec((None, tn, D_MODEL), lambda i, n: (l, n, 0)),
        ],
        out_specs=pl.BlockSpec(memory_space=pl.ANY),
        out_shape=jax.ShapeDtypeStruct((t, D_MODEL), F32),
        input_output_aliases={0: 0},
        scratch_shapes=[pltpu.VMEM((2, tm, D_MODEL), F32), pltpu.SemaphoreType.DMA((2,)), pltpu.SemaphoreType.DMA((2,))],
        compiler_params=_params(("arbitrary", "arbitrary")),
        name="mix_out",
    )(x, mod, ya, yb, yc, slab, slab, slab, w_branch, w_branch, w_branch, w_o)


def _rope_tables(seq):
    rows = seq // GRID_W
    row = jnp.repeat(jnp.arange(rows, dtype=F32), GRID_W)
    col = jnp.tile(jnp.arange(GRID_W, dtype=F32), rows)
    inv = ROPE_THETA ** (-jnp.arange(0, AXIS_DIM, 2, dtype=F32) / AXIS_DIM)
    ang_r, ang_c = row[:, None] * inv, col[:, None] * inv
    cos = jnp.concatenate([jnp.cos(ang_r)] * 2 + [jnp.cos(ang_c)] * 2, axis=1)
    sin = jnp.concatenate([-jnp.sin(ang_r), jnp.sin(ang_r), -jnp.sin(ang_c), jnp.sin(ang_c)], axis=1)
    return jnp.tile(cos, (1, LANES // DQK)), jnp.tile(sin, (1, LANES // DQK))


def kernel(x_prompt, x_sample, c, c_ctx, cache_k, cache_v, w_mod, b_mod, norm_g, w_ffn_in, w_ffn_out,
           w_in, w_conv, qk_gain, lam, subln_g, gmlp_ln_g, gmlp_ln_b, w_s, b_s, w_branch, w_o):
    batch, seq, _ = x_prompt.shape
    dec_batch, dec_seq, _ = x_sample.shape
    depth = w_mod.shape[0]
    past = cache_k.shape[2]
    bw = BRANCH_WIDTH
    assert dec_batch + 1 <= MOD_ROWS and seq % CHUNK == 0 and dec_seq % CHUNK == 0

    def ctx_tile(want):
        nb = max(1, want // seq)
        while batch % nb:
            nb -= 1
        return nb * seq

    tm_in = {"ctx": ctx_tile(512), "lat": min(1024, dec_seq)}
    tm_res = {"ctx": ctx_tile(1024), "lat": min(1024, dec_seq)}
    conv_rows = {"ctx": ctx_tile(2048), "lat": dec_seq}
    tf = 512
    tn_out = 512
    tc_conv = 256
    tq = min(512, dec_seq)

    cvec = jnp.zeros((MOD_ROWS, D_MODEL), F32).at[0].set(c_ctx).at[1:1 + dec_batch].set(c)
    mod = _mod_call(cvec, w_mod, b_mod).reshape(depth, MOD_ROWS, N_MOD, 1, D_MODEL)

    w_ffn_in_b = w_ffn_in.astype(BF16)
    w_ffn_out_b = w_ffn_out.astype(BF16)
    w_in_b = w_in.astype(BF16)
    w_branch_b = w_branch.astype(BF16)
    w_o_b = w_o.astype(BF16)
    w_s_b = w_s.astype(BF16)
    norm_g4 = norm_g.reshape(depth, 3, 1, D_MODEL)
    qk_gain_t = jnp.tile(qk_gain, (1, 1, bw // DQK)).reshape(depth, 2, 1, bw)
    ln_g3 = gmlp_ln_g.reshape(depth, 1, bw)
    ln_b3 = gmlp_ln_b.reshape(depth, 1, bw)
    b_s_t = jnp.repeat(jnp.swapaxes(b_s, 1, 2), bw // GMLP_GROUPS, axis=2)
    subg3 = subln_g.reshape(depth, 1, DV)
    rope = _rope_tables(dec_seq)
    cache_k4 = cache_k.reshape(dec_batch, depth, past, bw)
    cache_v4 = cache_v.reshape(dec_batch, depth, past, bw)

    def row_map(path, tm):
        if path == "ctx":
            return lambda i: 0
        tiles_per_seq = dec_seq // tm
        return lambda i: 1 + i // tiles_per_seq

    xp = x_prompt.reshape(batch * seq, D_MODEL)
    xs = x_sample.reshape(dec_batch * dec_seq, D_MODEL)
    new_caches = (jnp.zeros((batch, depth, seq, bw), F32), jnp.zeros((batch, depth, seq, bw), F32))

    for l in range(depth):
        for path in ("ctx", "lat"):
            is_ctx = path == "ctx"
            x = xp if is_ctx else xs
            sq = seq if is_ctx else dec_seq
            tmi, tmr = tm_in[path], tm_res[path]
            ffn = functools.partial(_ffn_call, mod=mod, norm_g=norm_g4, w_in=w_ffn_in_b, w_out=w_ffn_out_b,
                                    l=l, tm=tmr, tf=tf, row_of_tile=row_map(path, tmr))
            x = ffn(x, s=0, in_place=l > 0)
            outs = _mix_in_call(x, mod, norm_g4, w_in_b, qk_gain_t, None if is_ctx else rope, ln_g3, ln_b3, w_s_b, b_s_t,
                                new_caches if is_ctx else None, l=l, tm=tmi, seq=sq, row_of_tile=row_map(path, tmi))
            slab, q, k, yc = outs[:4]
            ya = _conv_call(slab, w_conv, l=l, rows=conv_rows[path], tc=tc_conv, seq=sq)
            if is_ctx:
                new_caches = (outs[4], outs[5])
                yb = _attn_ctx_call(q, k, slab, lam, subg3, l=l, seq=sq)
            else:
                yb = _attn_lat_call(q, k, slab, cache_k4, cache_v4, lam, subg3, l=l, seq=sq, tq=tq)
            x = _mix_out_call(x, mod, slab, ya, yb, yc, w_branch_b, w_o_b,
                              l=l, tm=tmr, tn=tn_out, row_of_tile=row_map(path, tmr))
            x = ffn(x, s=1)
            if is_ctx:
                xp = x
            else:
                xs = x

    return (xp.reshape(batch, seq, D_MODEL), xs.reshape(dec_batch, dec_seq, D_MODEL),
            new_caches[0].reshape(batch, depth, seq, N_HEADS, 2, DQK), new_caches[1].reshape(batch, depth, seq, N_HEADS, DV))
```

```python
import functools
import math

import jax
import jax.numpy as jnp
from jax import lax
from jax.experimental import pallas as pl
from jax.experimental.pallas import tpu as pltpu

D_MODEL = 2048
GRID_W = 64
N_MOD = 9
D_FF = 5632
BRANCH_WIDTH = D_MODEL // 2
N_HEADS = 8
DQK = 64
DV = 2 * DQK
GMLP_GROUPS = 8
CHUNK = 128
N_BRANCH = 3
ROPE_THETA = 10000.0
AXIS_DIM = DQK // 2
EPS = 1e-6
N_IN_GROUPS = 8 + 2 * N_BRANCH
LANES = 128
MXU_DIM = 256
MOD_ROWS = 16
MOD_COLS = 1024
VMEM_LIMIT = 56 * 1024 * 1024

G_XA, G_GB, G_GC, G_Q, G_K, G_V, G_U, G_VG, G_GATES = range(9)
SLOT_XA, SLOT_GB, SLOT_GC, SLOT_V, SLOT_GATES = 0, 1, 2, 3, 4
N_SLOTS = SLOT_GATES + 2 * N_BRANCH

F32 = jnp.float32
BF16 = jnp.bfloat16


def _slab_slot(j):
    return jnp.where(j < G_Q, j, jnp.where(j <= G_V, SLOT_V, jnp.where(j <= G_GATES, SLOT_GATES, j - G_GATES + SLOT_GATES)))


def _params(semantics):
    return pltpu.CompilerParams(dimension_semantics=semantics, vmem_limit_bytes=VMEM_LIMIT)


def _sigmoid(t):
    return 0.5 * jnp.tanh(0.5 * t) + 0.5


def _modulated_norm(x, gain, shift, scale):
    ms = jnp.mean(x * x, axis=-1, keepdims=True)
    return (x * lax.rsqrt(ms + EPS)) * (gain * (1.0 + scale)) + shift


def _mod_kernel(c_ref, w_ref, b_ref, o_ref):
    c = c_ref[...]
    a = (c * _sigmoid(c)).astype(BF16)
    o_ref[...] = jnp.dot(a, w_ref[...].astype(BF16), preferred_element_type=F32) + b_ref[...]


def _mod_call(cvec, w_mod, b_mod):
    depth = w_mod.shape[0]
    tn = MOD_COLS
    nn = N_MOD * D_MODEL // tn
    return pl.pallas_call(
        _mod_kernel,
        grid=(depth, nn),
        in_specs=[
            pl.BlockSpec((MOD_ROWS, D_MODEL), lambda l, n: (0, 0)),
            pl.BlockSpec((None, D_MODEL, tn), lambda l, n: (l, 0, n)),
            pl.BlockSpec((None, 1, tn), lambda l, n: (l, 0, n)),
        ],
        out_specs=pl.BlockSpec((None, MOD_ROWS, tn), lambda l, n: (l, 0, n)),
        out_shape=jax.ShapeDtypeStruct((depth, MOD_ROWS, N_MOD * D_MODEL), F32),
        compiler_params=_params(("parallel", "parallel")),
        name="adaln_mod",
    )(cvec, w_mod, b_mod.reshape(depth, 1, N_MOD * D_MODEL))


def _mod_spec(l, j, row_of_tile):
    return pl.BlockSpec((None, None, None, 1, D_MODEL), lambda i, n: (l, row_of_tile(i), j, 0, 0))


class _ResidualTiles:
    def __init__(self, x_hbm, o_hbm, buf_ref, sem_in, sem_out, tm):
        self.x_hbm, self.o_hbm, self.buf, self.sem_in, self.sem_out, self.tm = x_hbm, o_hbm, buf_ref, sem_in, sem_out, tm
        self.i, self.f = pl.program_id(0), pl.program_id(1)
        self.ntiles, self.nsteps = pl.num_programs(0), pl.num_programs(1)
        self.slot = self.i % 2

    def _rows(self, tile):
        return pl.ds(pl.multiple_of(tile * self.tm, self.tm), self.tm)

    def _load(self, tile, slot):
        return pltpu.make_async_copy(self.x_hbm.at[self._rows(tile)], self.buf.at[slot], self.sem_in.at[slot])

    def _store(self, tile, slot):
        return pltpu.make_async_copy(self.buf.at[slot], self.o_hbm.at[self._rows(tile)], self.sem_out.at[slot])

    def arrive(self):
        @pl.when((self.i == 0) & (self.f == 0))
        def _():
            self._load(0, 0).start()

        @pl.when(self.f == 0)
        def _():
            self._load(self.i, self.slot).wait()

    def prefetch_next(self):
        @pl.when((self.f == 1) & (self.i + 1 < self.ntiles))
        def _():
            @pl.when(self.i >= 1)
            def _():
                self._store(self.i - 1, 1 - self.slot).wait()
            self._load(self.i + 1, 1 - self.slot).start()

    def depart(self):
        last = self.f == self.nsteps - 1

        @pl.when(last)
        def _():
            self._store(self.i, self.slot).start()

        @pl.when(last & (self.i == self.ntiles - 1))
        def _():
            @pl.when(self.ntiles >= 2)
            def _():
                self._store(self.i - 1, 1 - self.slot).wait()
            self._store(self.i, self.slot).wait()


def _ffn_kernel(x_hbm, g_ref, sh_ref, sc_ref, gt_ref, wa_ref, wg_ref, wo_ref, o_hbm, res_ref, h_ref, sem_in, sem_out, *, tm):
    tiles = _ResidualTiles(x_hbm, o_hbm, res_ref, sem_in, sem_out, tm)
    tiles.arrive()
    acc = res_ref.at[tiles.slot]

    def accumulate(h):
        a = jnp.dot(h, wa_ref[...], preferred_element_type=F32)
        g = jnp.dot(h, wg_ref[...], preferred_element_type=F32)
        act = (a * _sigmoid(a) * g).astype(BF16)
        acc[...] += (0.5 * gt_ref[...]) * jnp.dot(act, wo_ref[...], preferred_element_type=F32)

    @pl.when(pl.program_id(1) == 0)
    def _():
        h = _modulated_norm(acc[...], g_ref[...], sh_ref[...], sc_ref[...]).astype(BF16)
        h_ref[...] = h
        accumulate(h)

    tiles.prefetch_next()

    @pl.when(pl.program_id(1) > 0)
    def _():
        accumulate(h_ref[...])

    tiles.depart()


def _ffn_call(x, mod, norm_g, w_in, w_out, *, l, s, tm, tf, row_of_tile, in_place=True):
    t = x.shape[0]
    nf = D_FF // tf
    assert nf >= 2 and t % tm == 0
    mod_base = 0 if s == 0 else 6
    norm_idx = 0 if s == 0 else 2
    return pl.pallas_call(
        functools.partial(_ffn_kernel, tm=tm),
        grid=(t // tm, nf),
        in_specs=[
            pl.BlockSpec(memory_space=pl.ANY),
            pl.BlockSpec((None, None, 1, D_MODEL), lambda i, f: (l, norm_idx, 0, 0)),
            _mod_spec(l, mod_base + 0, row_of_tile),
            _mod_spec(l, mod_base + 1, row_of_tile),
            _mod_spec(l, mod_base + 2, row_of_tile),
            pl.BlockSpec((None, None, D_MODEL, tf), lambda i, f: (l, s, 0, f)),
            pl.BlockSpec((None, None, D_MODEL, tf), lambda i, f: (l, s, 0, nf + f)),
            pl.BlockSpec((None, None, tf, D_MODEL), lambda i, f: (l, s, f, 0)),
        ],
        out_specs=pl.BlockSpec(memory_space=pl.ANY),
        out_shape=jax.ShapeDtypeStruct((t, D_MODEL), F32),
        input_output_aliases={0: 0} if in_place else {},
        scratch_shapes=[pltpu.VMEM((2, tm, D_MODEL), F32), pltpu.VMEM((tm, D_MODEL), BF16),
                        pltpu.SemaphoreType.DMA((2,)), pltpu.SemaphoreType.DMA((2,))],
        compiler_params=_params(("arbitrary", "arbitrary")),
        name=f"ffn{s}",
    )(x, norm_g, mod, mod, mod, w_in, w_in, w_out)


def _partner(y):
    lane = lax.broadcasted_iota(jnp.int32, y.shape, 1)
    ahead = pltpu.roll(y, LANES - AXIS_DIM // 2, axis=1)
    behind = pltpu.roll(y, AXIS_DIM // 2, axis=1)
    return jnp.where(lane % AXIS_DIM < AXIS_DIM // 2, ahead, behind)


def _segment_ones():
    r = lax.broadcasted_iota(jnp.int32, (MXU_DIM, MXU_DIM), 0)
    c = lax.broadcasted_iota(jnp.int32, (MXU_DIM, MXU_DIM), 1)
    return jnp.where(r // DQK == c // DQK, 1.0, 0.0).astype(BF16)


def _qk_norm(z, cos_g, sin_g, store):
    seg = _segment_ones()
    for pair in range(BRANCH_WIDTH // MXU_DIM):
        zp = z[:, pair * MXU_DIM:(pair + 1) * MXU_DIM]
        ss = jnp.dot((zp * zp).astype(BF16), seg, preferred_element_type=F32)
        r = lax.rsqrt(ss * (1.0 / DQK) + EPS)
        for half in range(MXU_DIM // LANES):
            zc = zp[:, half * LANES:(half + 1) * LANES]
            y = zc * cos_g
            if sin_g is not None:
                y = y + _partner(zc) * sin_g
            store(pair * (MXU_DIM // LANES) + half, y * r[:, half * LANES:(half + 1) * LANES])


def _mix_in_kernel(*refs, tm, seq, is_ctx):
    it = iter(refs)
    x_ref, g_ref, sh_ref, sc_ref, w_ref, qkg_ref = (next(it) for _ in range(6))
    cos_ref = sin_ref = None
    if not is_ctx:
        cos_ref, sin_ref = next(it), next(it)
    lng_ref, lnb_ref, ws_ref, bs_ref = (next(it) for _ in range(4))
    if is_ctx:
        next(it), next(it)
    slab_ref, q_ref, k_ref, yc_ref = (next(it) for _ in range(4))
    kc_ref = vc_ref = None
    if is_ctx:
        kc_ref, vc_ref = next(it), next(it)
    h_ref, ug_ref, xbuf_ref, xsem = (next(it) for _ in range(4))
    tab_ref = None if is_ctx else next(it)

    i, j = pl.program_id(0), pl.program_id(1)
    q_scale = DQK ** -0.5 * math.log2(math.e)

    def load_x(tile):
        rows = pl.ds(pl.multiple_of(tile * tm, tm), tm)
        return pltpu.make_async_copy(x_ref.at[rows], xbuf_ref, xsem.at[0])

    @pl.when((i == 0) & (j == 0))
    def _():
        load_x(0).start()

    @pl.when((j == 1) & (i + 1 < pl.num_programs(0)))
    def _():
        load_x(i + 1).start()

    def project():
        return jnp.dot(h_ref[...], w_ref[...], preferred_element_type=F32)

    @pl.when(j == G_XA)
    def _():
        load_x(i).wait()
        h = _modulated_norm(xbuf_ref[...], g_ref[...], sh_ref[...], sc_ref[...]).astype(BF16)
        h_ref[...] = h
        if not is_ctx:
            cos, sin = cos_ref[...], sin_ref[...]
            gq = jnp.broadcast_to(qkg_ref[0, :, :LANES], cos.shape) * q_scale
            gk = jnp.broadcast_to(qkg_ref[1, :, :LANES], cos.shape)
            tab_ref[0] = cos * gq
            tab_ref[1] = sin * _partner(gq)
            tab_ref[2] = cos * gk
            tab_ref[3] = sin * _partner(gk)
        slab_ref[...] = jnp.dot(h, w_ref[...], preferred_element_type=F32).astype(BF16)

    cast_only = (j == G_GB) | (j == G_GC) | (j >= G_GATES)
    if not is_ctx:
        cast_only = cast_only | (j == G_V)

    @pl.when(cast_only)
    def _():
        slab_ref[...] = project().astype(BF16)

    if is_ctx:
        @pl.when(j == G_V)
        def _():
            z = project()
            slab_ref[...] = z.astype(BF16)
            for b in range(tm // seq):
                vc_ref[b] = z[b * seq:(b + 1) * seq]

    def store_q(c, y):
        q_ref[:, c * LANES:(c + 1) * LANES] = y.astype(BF16)

    def store_k(c, y):
        k_ref[:, c * LANES:(c + 1) * LANES] = y.astype(BF16)
        if is_ctx:
            for b in range(tm // seq):
                kc_ref[b, :, c * LANES:(c + 1) * LANES] = y[b * seq:(b + 1) * seq]

    @pl.when(j == G_Q)
    def _():
        if is_ctx:
            _qk_norm(project(), qkg_ref[0, :, :LANES] * q_scale, None, store_q)
        else:
            _qk_norm(project(), tab_ref[0], tab_ref[1], store_q)

    @pl.when(j == G_K)
    def _():
        if is_ctx:
            _qk_norm(project(), qkg_ref[1, :, :LANES], None, store_k)
        else:
            _qk_norm(project(), tab_ref[2], tab_ref[3], store_k)

    @pl.when(j == G_U)
    def _():
        ug_ref[...] = jax.nn.gelu(project())

    @pl.when(j == G_VG)
    def _():
        gv = jax.nn.gelu(project())
        mu = jnp.mean(gv, axis=-1, keepdims=True)
        d = gv - mu
        var = jnp.mean(d * d, axis=-1, keepdims=True)
        vn = (d * lax.rsqrt(var + EPS) * lng_ref[...] + lnb_ref[...]).astype(BF16)
        nch = tm // CHUNK
        for g in range(GMLP_GROUPS):
            cols = slice(g * CHUNK, (g + 1) * CHUNK)
            stacked = jnp.concatenate([vn[n * CHUNK:(n + 1) * CHUNK, cols] for n in range(nch)], axis=1)
            mixed = jnp.dot(ws_ref[g], stacked, preferred_element_type=F32)
            for n in range(nch):
                rows = slice(n * CHUNK, (n + 1) * CHUNK)
                m = mixed[:, n * CHUNK:(n + 1) * CHUNK] + bs_ref[:, cols]
                yc_ref[rows, cols] = (ug_ref[rows, cols] * m).astype(BF16)


def _mix_in_call(x, mod, norm_g, w_in, qk_gain_t, rope, ln_g, ln_b, w_s, b_s_t, caches, *, l, tm, seq, row_of_tile):
    t = x.shape[0]
    is_ctx = caches is not None
    bw = BRANCH_WIDTH
    tok = lambda i, j: (i, 0)
    in_specs = [
        pl.BlockSpec(memory_space=pl.ANY),
        pl.BlockSpec((None, None, 1, D_MODEL), lambda i, j: (l, 1, 0, 0)),
        _mod_spec(l, 3, row_of_tile),
        _mod_spec(l, 4, row_of_tile),
        pl.BlockSpec((None, D_MODEL, bw), lambda i, j: (l, 0, j)),
        pl.BlockSpec((None, 2, 1, bw), lambda i, j: (l, 0, 0, 0)),
    ]
    args = [x, norm_g, mod, mod, w_in, qk_gain_t]
    scratch = [pltpu.VMEM((tm, D_MODEL), BF16), pltpu.VMEM((tm, bw), F32),
               pltpu.VMEM((tm, D_MODEL), F32), pltpu.SemaphoreType.DMA((1,))]
    if not is_ctx:
        tiles_per_seq = seq // tm
        in_specs += [pl.BlockSpec((tm, LANES), lambda i, j: (i % tiles_per_seq, 0))] * 2
        args += [rope[0], rope[1]]
        scratch += [pltpu.VMEM((4, tm, LANES), F32)]
    in_specs += [
        pl.BlockSpec((None, 1, bw), lambda i, j: (l, 0, 0)),
        pl.BlockSpec((None, 1, bw), lambda i, j: (l, 0, 0)),
        pl.BlockSpec((None, GMLP_GROUPS, CHUNK, CHUNK), lambda i, j: (l, 0, 0, 0)),
        pl.BlockSpec((None, CHUNK, bw), lambda i, j: (l, 0, 0)),
    ]
    args += [ln_g, ln_b, w_s, b_s_t]
    act = jax.ShapeDtypeStruct((t, bw), BF16)
    out_shape = [jax.ShapeDtypeStruct((t, N_SLOTS * bw), BF16), act, act, act]
    out_specs = [pl.BlockSpec((tm, bw), lambda i, j: (i, _slab_slot(j)))] + [pl.BlockSpec((tm, bw), tok)] * 3
    aliases = {}
    if is_ctx:
        nb = tm // seq
        cache_spec = pl.BlockSpec((nb, None, seq, bw), lambda i, j: (i, l, 0, 0))
        in_specs += [pl.BlockSpec(memory_space=pl.ANY)] * 2
        aliases = {len(args): len(out_shape), len(args) + 1: len(out_shape) + 1}
        args += list(caches)
        out_shape += [jax.ShapeDtypeStruct(c.shape, c.dtype) for c in caches]
        out_specs += [cache_spec, cache_spec]
    return pl.pallas_call(
        functools.partial(_mix_in_kernel, tm=tm, seq=seq, is_ctx=is_ctx),
        grid=(t // tm, N_IN_GROUPS),
        in_specs=in_specs,
        out_specs=out_specs,
        out_shape=out_shape,
        input_output_aliases=aliases,
        scratch_shapes=scratch,
        compiler_params=_params(("arbitrary", "arbitrary")),
        name="mix_in_ctx" if is_ctx else "mix_in_lat",
    )(*args)


def _lambda_init(l):
    return 0.8 - 0.6 * math.exp(-0.3 * l)


def _lambda_full(lam_ref, l):
    lp = lam_ref[...]
    a = jnp.sum(lp[0:1] * lp[1:2], axis=-1, keepdims=True)
    b = jnp.sum(lp[2:3] * lp[3:4], axis=-1, keepdims=True)
    return jnp.exp(a) - jnp.exp(b) + _lambda_init(l)


def _softmax_times_values(qm, k, vaug):
    s = lax.dot_general(qm, k, (((1,), (1,)), ((), ())), preferred_element_type=F32)
    e = jnp.exp2(s - jnp.max(s, axis=-1, keepdims=True)).astype(BF16)
    pv = jnp.dot(e, vaug, preferred_element_type=F32)
    return pv[:, :DV] / pv[:, DV:]


def _attend_head(qh, kh, vaug, lam_full, subg, l):
    lane = lax.broadcasted_iota(jnp.int32, qh.shape, 1)
    zero = jnp.zeros_like(qh)
    o0 = _softmax_times_values(jnp.where(lane < DQK, qh, zero), kh, vaug)
    o1 = _softmax_times_values(jnp.where(lane < DQK, zero, qh), kh, vaug)
    o = o0 - lam_full * o1
    ms = jnp.mean(o * o, axis=-1, keepdims=True)
    return (o * lax.rsqrt(ms + EPS) * subg) * (1.0 - _lambda_init(l))


def _attn_ctx_kernel(q_ref, k_ref, v_ref, lam_ref, subg_ref, o_ref, *, l):
    lam_full = _lambda_full(lam_ref, l)
    ones = jnp.ones((k_ref.shape[0], DV), BF16)
    for h in range(N_HEADS):
        cols = slice(h * DV, (h + 1) * DV)
        vaug = jnp.concatenate([v_ref[:, cols], ones], axis=1)
        o_ref[:, cols] = _attend_head(q_ref[:, cols], k_ref[:, cols], vaug, lam_full, subg_ref[...], l).astype(BF16)


def _attn_ctx_call(q, k, slab, lam, subln_g, *, l, seq):
    t = q.shape[0]
    blk = pl.BlockSpec((seq, BRANCH_WIDTH), lambda b: (b, 0))
    return pl.pallas_call(
        functools.partial(_attn_ctx_kernel, l=l),
        grid=(t // seq,),
        in_specs=[blk, blk, pl.BlockSpec((seq, BRANCH_WIDTH), lambda b: (b, SLOT_V)),
                  pl.BlockSpec((None, 4, DQK), lambda b: (l, 0, 0)),
                  pl.BlockSpec((None, 1, DV), lambda b: (l, 0, 0))],
        out_specs=blk,
        out_shape=jax.ShapeDtypeStruct((t, BRANCH_WIDTH), BF16),
        compiler_params=_params(("parallel",)),
        name="attn_ctx",
    )(q, k, slab, lam, subln_g)


def _attn_lat_kernel(q_ref, k_ref, v_ref, ck_ref, cv_ref, lam_ref, subg_ref, o_ref, kall_ref, vaug_ref, *, l, past):
    @pl.when(pl.program_id(1) == 0)
    def _():
        kall_ref[:past] = ck_ref[...].astype(BF16)
        kall_ref[past:] = k_ref[...]
        for h in range(N_HEADS):
            cols = slice(h * DV, (h + 1) * DV)
            vaug_ref[h, :past, :DV] = cv_ref[:, cols].astype(BF16)
            vaug_ref[h, past:, :DV] = v_ref[:, cols]
            vaug_ref[h, :, DV:] = jnp.ones((vaug_ref.shape[1], DV), BF16)

    lam_full = _lambda_full(lam_ref, l)
    for h in range(N_HEADS):
        cols = slice(h * DV, (h + 1) * DV)
        o_ref[:, cols] = _attend_head(q_ref[:, cols], kall_ref[:, cols], vaug_ref[h], lam_full, subg_ref[...], l).astype(BF16)


def _attn_lat_call(q, k, slab, cache_k, cache_v, lam, subln_g, *, l, seq, tq):
    t = q.shape[0]
    past = cache_k.shape[2]
    nq = seq // tq
    bw = BRANCH_WIDTH
    cache = pl.BlockSpec((None, None, past, bw), lambda b, i: (b, l, 0, 0))
    return pl.pallas_call(
        functools.partial(_attn_lat_kernel, l=l, past=past),
        grid=(t // seq, nq),
        in_specs=[pl.BlockSpec((tq, bw), lambda b, i: (b * nq + i, 0)),
                  pl.BlockSpec((seq, bw), lambda b, i: (b, 0)),
                  pl.BlockSpec((seq, bw), lambda b, i: (b, SLOT_V)),
                  cache, cache,
                  pl.BlockSpec((None, 4, DQK), lambda b, i: (l, 0, 0)),
                  pl.BlockSpec((None, 1, DV), lambda b, i: (l, 0, 0))],
        out_specs=pl.BlockSpec((tq, bw), lambda b, i: (b * nq + i, 0)),
        out_shape=jax.ShapeDtypeStruct((t, bw), BF16),
        scratch_shapes=[pltpu.VMEM((past + seq, bw), BF16), pltpu.VMEM((N_HEADS, past + seq, 2 * DV), BF16)],
        compiler_params=_params(("parallel", "arbitrary")),
        name="attn_lat",
    )(q, k, slab, cache_k, cache_v, lam, subln_g)


def _conv_kernel(xa_ref, gc_ref, gb_ref, wc_ref, o_ref, *, seq):
    p = xa_ref[...].astype(F32) * gc_ref[...].astype(F32)
    rows = p.shape[0]
    pos = lax.broadcasted_iota(jnp.int32, p.shape, 0) % seq
    prev = jnp.where(pos == 0, 0.0, pltpu.roll(p, 1, axis=0))
    nxt = jnp.where(pos == seq - 1, 0.0, pltpu.roll(p, rows - 1, axis=0))
    conv = wc_ref[0:1, :] * prev + wc_ref[1:2, :] * p + wc_ref[2:3, :] * nxt
    o_ref[...] = (gb_ref[...].astype(F32) * conv).astype(BF16)


def _conv_call(slab, w_conv, *, l, rows, tc, seq):
    t = slab.shape[0]
    bw = BRANCH_WIDTH
    nc = bw // tc
    assert rows % seq == 0 and t % rows == 0

    def col(slot):
        return pl.BlockSpec((rows, tc), lambda i, c: (i, slot * nc + c))

    return pl.pallas_call(
        functools.partial(_conv_kernel, seq=seq),
        grid=(t // rows, nc),
        in_specs=[col(SLOT_XA), col(SLOT_GC), col(SLOT_GB), pl.BlockSpec((None, 3, tc), lambda i, c: (l, 0, c))],
        out_specs=pl.BlockSpec((rows, tc), lambda i, c: (i, c)),
        out_shape=jax.ShapeDtypeStruct((t, bw), BF16),
        compiler_params=_params(("parallel", "parallel")),
        name="short_conv",
    )(slab, slab, slab, w_conv)


def _mix_out_kernel(x_hbm, gt_ref, ya_ref, yb_ref, yc_ref, za_ref, zb_ref, zc_ref, wa_ref, wb_ref, wcc_ref, wo_ref,
                    o_hbm, res_ref, sem_in, sem_out, *, tm):
    tiles = _ResidualTiles(x_hbm, o_hbm, res_ref, sem_in, sem_out, tm)
    tiles.arrive()
    tiles.prefetch_next()
    acc = res_ref.at[tiles.slot]
    ya = jnp.dot(ya_ref[...], wa_ref[...], preferred_element_type=F32)
    yb = jnp.dot(yb_ref[...], wb_ref[...], preferred_element_type=F32)
    yc = jnp.dot(yc_ref[...], wcc_ref[...], preferred_element_type=F32)
    merged = (_sigmoid(za_ref[...].astype(F32)) * ya + _sigmoid(zb_ref[...].astype(F32)) * yb
              + _sigmoid(zc_ref[...].astype(F32)) * yc)
    acc[...] += gt_ref[...] * jnp.dot(merged.astype(BF16), wo_ref[...], preferred_element_type=F32)
    tiles.depart()


def _mix_out_call(x, mod, slab, ya, yb, yc, w_branch, w_o, *, l, tm, tn, row_of_tile):
    t = x.shape[0]
    bw = BRANCH_WIDTH
    nn = D_MODEL // tn
    assert nn >= 2 and t % tm == 0
    gate0 = SLOT_GATES * bw // tn

    def gate(branch):
        return pl.BlockSpec((tm, tn), lambda i, n: (i, gate0 + branch * nn + n))

    def branch_w(branch):
        return pl.BlockSpec((None, None, bw, tn), lambda i, n: (l, branch, 0, n))

    act = pl.BlockSpec((tm, bw), lambda i, n: (i, 0))
    return pl.pallas_call(
        functools.partial(_mix_out_kernel, tm=tm),
        grid=(t // tm, nn),
        in_specs=[
            pl.BlockSpec(memory_space=pl.ANY),
            _mod_spec(l, 5, row_of_tile),
            act, act, act,
            gate(0), gate(1), gate(2),
            branch_w(0), branch_w(1), branch_w(2),
            pl.BlockSpec((None, tn, D_MODEL), lambda i, n: (l, n, 0)),
        ],
        out_specs=pl.BlockSpec(memory_space=pl.ANY),
        out_shape=jax.ShapeDtypeStruct((t, D_MODEL), F32),
        input_output_aliases={0: 0},
        scratch_shapes=[pltpu.VMEM((2, tm, D_MODEL), F32), pltpu.SemaphoreType.DMA((2,)), pltpu.SemaphoreType.DMA((2,))],
        compiler_params=_params(("arbitrary", "arbitrary")),
        name="mix_out",
    )(x, mod, ya, yb, yc, slab, slab, slab, w_branch, w_branch, w_branch, w_o)


def _rope_tables(seq):
    rows = seq // GRID_W
    row = jnp.repeat(jnp.arange(rows, dtype=F32), GRID_W)
    col = jnp.tile(jnp.arange(GRID_W, dtype=F32), rows)
    inv = ROPE_THETA ** (-jnp.arange(0, AXIS_DIM, 2, dtype=F32) / AXIS_DIM)
    ang_r, ang_c = row[:, None] * inv, col[:, None] * inv
    cos = jnp.concatenate([jnp.cos(ang_r)] * 2 + [jnp.cos(ang_c)] * 2, axis=1)
    sin = jnp.concatenate([-jnp.sin(ang_r), jnp.sin(ang_r), -jnp.sin(ang_c), jnp.sin(ang_c)], axis=1)
    return jnp.tile(cos, (1, LANES // DQK)), jnp.tile(sin, (1, LANES // DQK))


def kernel(x_prompt, x_sample, c, c_ctx, cache_k, cache_v, w_mod, b_mod, norm_g, w_ffn_in, w_ffn_out,
           w_in, w_conv, qk_gain, lam, subln_g, gmlp_ln_g, gmlp_ln_b, w_s, b_s, w_branch, w_o):
    batch, seq, _ = x_prompt.shape
    dec_batch, dec_seq, _ = x_sample.shape
    depth = w_mod.shape[0]
    past = cache_k.shape[2]
    bw = BRANCH_WIDTH
    assert dec_batch + 1 <= MOD_ROWS and seq % CHUNK == 0 and dec_seq % CHUNK == 0

    def ctx_tile(want):
        nb = max(1, want // seq)
        while batch % nb:
            nb -= 1
        return nb * seq

    tm_in = {"ctx": ctx_tile(512), "lat": min(1024, dec_seq)}
    tm_res = {"ctx": ctx_tile(1024), "lat": min(1024, dec_seq)}
    conv_rows = {"ctx": ctx_tile(2048), "lat": dec_seq}
    tf = 512
    tn_out = 512
    tc_conv = 256
    tq = min(512, dec_seq)

    cvec = jnp.zeros((MOD_ROWS, D_MODEL), F32).at[0].set(c_ctx).at[1:1 + dec_batch].set(c)
    mod = _mod_call(cvec, w_mod, b_mod).reshape(depth, MOD_ROWS, N_MOD, 1, D_MODEL)

    w_ffn_in_b = w_ffn_in.astype(BF16)
    w_ffn_out_b = w_ffn_out.astype(BF16)
    w_in_b = w_in.astype(BF16)
    w_branch_b = w_branch.astype(BF16)
    w_o_b = w_o.astype(BF16)
    w_s_b = w_s.astype(BF16)
    norm_g4 = norm_g.reshape(depth, 3, 1, D_MODEL)
    qk_gain_t = jnp.tile(qk_gain, (1, 1, bw // DQK)).reshape(depth, 2, 1, bw)
    ln_g3 = gmlp_ln_g.reshape(depth, 1, bw)
    ln_b3 = gmlp_ln_b.reshape(depth, 1, bw)
    b_s_t = jnp.repeat(jnp.swapaxes(b_s, 1, 2), bw // GMLP_GROUPS, axis=2)
    subg3 = subln_g.reshape(depth, 1, DV)
    rope = _rope_tables(dec_seq)
    cache_k4 = cache_k.reshape(dec_batch, depth, past, bw)
    cache_v4 = cache_v.reshape(dec_batch, depth, past, bw)

    def row_map(path, tm):
        if path == "ctx":
            return lambda i: 0
        tiles_per_seq = dec_seq // tm
        return lambda i: 1 + i // tiles_per_seq

    xp = x_prompt.reshape(batch * seq, D_MODEL)
    xs = x_sample.reshape(dec_batch * dec_seq, D_MODEL)
    new_caches = (jnp.zeros((batch, depth, seq, bw), F32), jnp.zeros((batch, depth, seq, bw), F32))

    for l in range(depth):
        for path in ("ctx", "lat"):
            is_ctx = path == "ctx"
            x = xp if is_ctx else xs
            sq = seq if is_ctx else dec_seq
            tmi, tmr = tm_in[path], tm_res[path]
            ffn = functools.partial(_ffn_call, mod=mod, norm_g=norm_g4, w_in=w_ffn_in_b, w_out=w_ffn_out_b,
                                    l=l, tm=tmr, tf=tf, row_of_tile=row_map(path, tmr))
            x = ffn(x, s=0, in_place=l > 0)
            outs = _mix_in_call(x, mod, norm_g4, w_in_b, qk_gain_t, None if is_ctx else rope, ln_g3, ln_b3, w_s_b, b_s_t,
                                new_caches if is_ctx else None, l=l, tm=tmi, seq=sq, row_of_tile=row_map(path, tmi))
            slab, q, k, yc = outs[:4]
            ya = _conv_call(slab, w_conv, l=l, rows=conv_rows[path], tc=tc_conv, seq=sq)
            if is_ctx:
                new_caches = (outs[4], outs[5])
                yb = _attn_ctx_call(q, k, slab, lam, subg3, l=l, seq=sq)
            else:
                yb = _attn_lat_call(q, k, slab, cache_k4, cache_v4, lam, subg3, l=l, seq=sq, tq=tq)
            x = _mix_out_call(x, mod, slab, ya, yb, yc, w_branch_b, w_o_b,
                              l=l, tm=tmr, tn=tn_out, row_of_tile=row_map(path, tmr))
            x = ffn(x, s=1)
            if is_ctx:
                xp = x
            else:
                xs = x

    return (xp.reshape(batch, seq, D_MODEL), xs.reshape(dec_batch, dec_seq, D_MODEL),
            new_caches[0].reshape(batch, depth, seq, N_HEADS, 2, DQK), new_caches[1].reshape(batch, depth, seq, N_HEADS, DV))
```
